```python
import jax, jax.numpy as jnp
from jax import lax
import numpy as np

D_MODEL = 2048
BATCH = 4
SEQ = 8192
DEPTH = 1

ATTN_WIDTH = D_MODEL // 2
ATTN_HEAD_DIM = 128
ATTN_HEADS = ATTN_WIDTH // ATTN_HEAD_DIM
DILATED_PATTERNS = ((128, 1), (512, 4), (2048, 16))
ROPE_THETA = 500000.0
ROPE_DIM = ATTN_HEAD_DIM // 4

GLA_VALUE_WIDTH = D_MODEL - ATTN_WIDTH
GLA_KEY_WIDTH = GLA_VALUE_WIDTH // 2
GLA_HEADS = 4
GLA_DK = GLA_KEY_WIDTH // GLA_HEADS
GLA_DV = GLA_VALUE_WIDTH // GLA_HEADS
GLA_GATE_RANK = 16
GLA_GATE_NORMALIZER = 16.0
GLA_CHUNK = 64

IN_SPLITS = (ATTN_WIDTH, ATTN_WIDTH, ATTN_WIDTH,
             GLA_KEY_WIDTH, GLA_KEY_WIDTH, GLA_VALUE_WIDTH, GLA_VALUE_WIDTH,
             GLA_GATE_RANK, GLA_GATE_RANK)
IN_WIDTH = sum(IN_SPLITS)

D_FF = 5632
CONV_WIDTH = 3
EPS = 1e-6

kernel_name = "hymba_gla_dilated_attn_convglu_encoder"


def rmsnorm(x, g):
    xf = x.astype(jnp.float32)
    y = xf * lax.rsqrt(jnp.mean(xf * xf, axis=-1, keepdims=True) + EPS)
    return (y * g.astype(jnp.float32)).astype(x.dtype)


def apply_partial_rope(t):
    S = t.shape[1]
    pos = jnp.arange(S, dtype=jnp.float32)
    inv_freq = ROPE_THETA ** (-jnp.arange(0, ROPE_DIM, 2, dtype=jnp.float32) / ROPE_DIM)
    ang = pos[:, None] * inv_freq[None, :]
    cos = jnp.cos(ang)[None, :, None, :]
    sin = jnp.sin(ang)[None, :, None, :]
    tr = t[..., :ROPE_DIM].astype(jnp.float32)
    x1, x2 = tr[..., :ROPE_DIM // 2], tr[..., ROPE_DIM // 2:]
    rot = jnp.concatenate([x1 * cos - x2 * sin, x2 * cos + x1 * sin], axis=-1)
    return jnp.concatenate([rot.astype(t.dtype), t[..., ROPE_DIM:]], axis=-1)


def to_residues(t, d):
    B, S = t.shape[:2]
    rest = t.shape[2:]
    t = t.reshape((B, S // d, d) + rest)
    return jnp.swapaxes(t, 1, 2).reshape((B * d, S // d) + rest)


def from_residues(t, d, B):
    N, L = t.shape[:2]
    rest = t.shape[2:]
    t = t.reshape((B, d, L) + rest)
    return jnp.swapaxes(t, 1, 2).reshape((B, L * d) + rest)


def banded_attention(q, k, v, n_side):
    N, L, H, Dh = q.shape
    blk = n_side
    nb = -(-L // blk)
    Lp = nb * blk
    pad = Lp - L
    qb = jnp.pad(q, ((0, 0), (0, pad), (0, 0), (0, 0))).reshape(N, nb, blk, H, Dh)
    kp = jnp.pad(k, ((0, 0), (blk, pad + blk), (0, 0), (0, 0)))
    vp = jnp.pad(v, ((0, 0), (blk, pad + blk), (0, 0), (0, 0)))

    def key_blocks(t):
        return jnp.concatenate(
            [t[:, i * blk:i * blk + Lp].reshape(N, nb, blk, H, Dh) for i in range(3)], axis=2)

    kb, vb = key_blocks(kp), key_blocks(vp)
    s = jnp.einsum('nbqhd,nbkhd->nbhqk', qb, kb).astype(jnp.float32) * (Dh ** -0.5)
    qpos = jnp.arange(nb)[:, None] * blk + jnp.arange(blk)[None, :]
    kpos = jnp.arange(nb)[:, None] * blk - blk + jnp.arange(3 * blk)[None, :]
    qp_, kp_ = qpos[:, :, None], kpos[:, None, :]
    mask = (jnp.abs(kp_ - qp_) <= n_side) & (kp_ >= 0) & ((kp_ < L) | (kp_ == qp_))
    s = jnp.where(mask[None, :, None], s, -jnp.inf)
    m = jnp.max(s, axis=-1, keepdims=True)
    p = jnp.exp(s - m)
    den = jnp.sum(p, axis=-1, keepdims=True)
    lse = (m + jnp.log(den))[..., 0]
    p = p / den
    o = jnp.einsum('nbhqk,nbkhd->nbqhd', p.astype(vb.dtype), vb).reshape(N, Lp, H, Dh)[:, :L]
    lse = jnp.swapaxes(lse, 2, 3).reshape(N, Lp, H)[:, :L]
    return o, lse


def dilated_attention(q, k, v):
    B = q.shape[0]
    outs, lses = [], []
    for window, d in DILATED_PATTERNS:
        n_side = (window // 2) // d
        o, lse = banded_attention(to_residues(q, d), to_residues(k, d), to_residues(v, d), n_side)
        outs.append(from_residues(o, d, B).astype(jnp.float32))
        lses.append(from_residues(lse, d, B))
    w = jax.nn.softmax(jnp.stack(lses), axis=0)
    o = jnp.einsum('pbsh,pbshd->bshd', w, jnp.stack(outs))
    return o.astype(q.dtype)


def gla_chunked(q, k, v, g):
    B, S, H, K = q.shape
    V = v.shape[-1]
    C = GLA_CHUNK
    n = S // C
    q, k, v, g = (t.reshape(B, n, C, H, t.shape[-1]) for t in (q, k, v, g))
    b = jnp.cumsum(g, axis=2)
    b_ref = b[:, :, C // 2 - 1:C // 2]
    b_last = b[:, :, -1]
    a = jnp.einsum('bnihk,bnjhk->bnhij', q * jnp.exp(b - b_ref), k * jnp.exp(b_ref - b))
    a = jnp.where(jnp.tril(jnp.ones((C, C), dtype=bool)), a, 0.0)
    o_intra = jnp.einsum('bnhij,bnjhv->bnihv', a, v)
    q_in = q * jnp.exp(b)
    k_st = k * jnp.exp(b_last[:, :, None] - b)
    dec = jnp.exp(b_last)

    def step(state, xs):
        qc, kc, vc, dc = xs
        o = jnp.einsum('bihk,bhkv->bihv', qc, state)
        state = dc[..., None] * state + jnp.einsum('bjhk,bjhv->bhkv', kc, vc)
        return state, o

    xs = tuple(jnp.moveaxis(t, 1, 0) for t in (q_in, k_st, v, dec))
    _, o_inter = lax.scan(step, jnp.zeros((B, H, K, V), jnp.float32), xs)
    o = o_intra + jnp.moveaxis(o_inter, 0, 1)
    return o.reshape(B, S, H, V)


def hybrid_mixer(h, w_in, gf_up, gf_b, gb_up, gb_b, gla_norm_g, attn_norm_g, w_out):
    B, S, _ = h.shape
    f32 = jnp.float32
    proj = h @ w_in
    split_at = np.cumsum(IN_SPLITS)[:-1].tolist()
    aq, ak, av, gq, gk, gv, gr, zf, zb = jnp.split(proj, split_at, axis=-1)

    def heads(t, nh):
        return t.reshape(B, S, nh, -1)

    aq = apply_partial_rope(heads(aq, ATTN_HEADS))
    ak = apply_partial_rope(heads(ak, ATTN_HEADS))
    ao = dilated_attention(aq, ak, heads(av, ATTN_HEADS)).reshape(B, S, ATTN_WIDTH)
    ao = rmsnorm(ao, attn_norm_g)

    q = heads(gq, GLA_HEADS).astype(f32) * (GLA_DK ** -0.5)
    k = heads(gk, GLA_HEADS).astype(f32)
    v = heads(gv, GLA_HEADS).astype(f32)
    log_gf = jax.nn.log_sigmoid((zf @ gf_up + gf_b).astype(f32)) / GLA_GATE_NORMALIZER
    log_gb = jax.nn.log_sigmoid((zb @ gb_up + gb_b).astype(f32)) / GLA_GATE_NORMALIZER
    flip = lambda t: jnp.flip(t, axis=1)
    o_f = gla_chunked(q, k, v, heads(log_gf, GLA_HEADS))
    o_b = flip(gla_chunked(flip(q), flip(k), flip(v), flip(heads(log_gb, GLA_HEADS))))
    go = rmsnorm(o_f + o_b, gla_norm_g).astype(h.dtype)
    go = (go * jax.nn.silu(heads(gr, GLA_HEADS))).reshape(B, S, GLA_VALUE_WIDTH)

    return jnp.concatenate([ao, go], axis=-1) @ w_out


def conv_glu_ffn(h, w_gate, w_up, conv_w, conv_b, w_down):
    gate = h @ w_gate
    gate = lax.conv_general_dilated(
        gate, conv_w[:, None, :].astype(gate.dtype), window_strides=(1,),
        padding=((CONV_WIDTH // 2, CONV_WIDTH // 2),),
        dimension_numbers=('NWC', 'WIO', 'NWC'), feature_group_count=D_FF) + conv_b
    return (jax.nn.silu(gate) * (h @ w_up)) @ w_down


def setup_inputs(seed: int = 0) -> dict:
    key = jax.random.key(seed)
    ks = jax.random.split(key, 20)
    nrm = lambda k, shape, scale: jax.random.normal(k, shape, jnp.float32) * scale
    gain = lambda k, shape: 1.0 + 0.02 * jax.random.normal(k, shape, jnp.float32)
    L = DEPTH
    return {
        "x": jax.random.normal(ks[0], (BATCH, SEQ, D_MODEL), jnp.float32),
        "norm1_g": gain(ks[1], (L, D_MODEL)),
        "w_in": nrm(ks[2], (L, D_MODEL, IN_WIDTH), D_MODEL ** -0.5),
        "gf_up": nrm(ks[3], (L, GLA_GATE_RANK, GLA_KEY_WIDTH), GLA_GATE_RANK ** -0.5),
        "gf_b": nrm(ks[4], (L, GLA_KEY_WIDTH), 0.1),
        "gb_up": nrm(ks[5], (L, GLA_GATE_RANK, GLA_KEY_WIDTH), GLA_GATE_RANK ** -0.5),
        "gb_b": nrm(ks[6], (L, GLA_KEY_WIDTH), 0.1),
        "gla_norm_g": gain(ks[7], (L, GLA_DV)),
        "attn_norm_g": gain(ks[8], (L, ATTN_WIDTH)),
        "w_out": nrm(ks[9], (L, D_MODEL, D_MODEL), D_MODEL ** -0.5),
        "norm2_g": gain(ks[10], (L, D_MODEL)),
        "w_gate": nrm(ks[11], (L, D_MODEL, D_FF), D_MODEL ** -0.5),
        "w_up": nrm(ks[12], (L, D_MODEL, D_FF), D_MODEL ** -0.5),
        "conv_w": nrm(ks[13], (L, CONV_WIDTH, D_FF), CONV_WIDTH ** -0.5),
        "conv_b": nrm(ks[14], (L, D_FF), 0.02),
        "w_down": nrm(ks[15], (L, D_FF, D_MODEL), D_FF ** -0.5),
        "final_norm_g": gain(ks[16], (D_MODEL,)),
    }


def reference(x, norm1_g, w_in, gf_up, gf_b, gb_up, gb_b, gla_norm_g, attn_norm_g, w_out,
              norm2_g, w_gate, w_up, conv_w, conv_b, w_down, final_norm_g):
    h = x
    for l in range(DEPTH):
        h = h + hybrid_mixer(rmsnorm(h, norm1_g[l]), w_in[l], gf_up[l], gf_b[l], gb_up[l], gb_b[l],
                             gla_norm_g[l], attn_norm_g[l], w_out[l])
        h = h + conv_glu_ffn(rmsnorm(h, norm2_g[l]), w_gate[l], w_up[l], conv_w[l], conv_b[l], w_down[l])
    return rmsnorm(h, final_norm_g)
```

```python
import functools

import jax
import jax.numpy as jnp
from jax import lax
from jax.experimental import pallas as pl
from jax.experimental.pallas import tpu as pltpu

F32 = jnp.float32
BF16 = jnp.bfloat16

D_MODEL = 2048
ATTN_WIDTH = 1024
HEAD_DIM = 128
ATTN_HEADS = ATTN_WIDTH // HEAD_DIM
DILATED_PATTERNS = ((128, 1), (512, 4), (2048, 16))
N_SIDE = 64
ROPE_THETA = 500000.0
ROPE_DIM = HEAD_DIM // 4
ROPE_HALF = ROPE_DIM // 2
GLA_HEADS = 4
GLA_DK = 128
GLA_DV = 256
GLA_KEY_WIDTH = GLA_HEADS * GLA_DK
GLA_VALUE_WIDTH = GLA_HEADS * GLA_DV
GATE_RANK = 16
GATE_NORMALIZER = 16.0
GLA_CHUNK = 64
D_FF = 5632
CONV_WIDTH = 3
EPS = 1e-6

LANES = 128
BF16_SUBLANES = 16
VMEM_LIMIT_BYTES = 56 * 1024 * 1024

INPROJ_ROWS = 1024
INPROJ_COLS = 512
ATTN_ROWS = 512
ATTN_QBLK = 128
GLA_ROWS = 512
GLA_PAIR = 2 * GLA_CHUNK
OUTPROJ_ROWS = 256
FFN_ROWS = 512
FFN_COLS = 512
FFN_HALO = BF16_SUBLANES

ATTN_PROJ_WIDTH = 3 * ATTN_WIDTH
GLA_PROJ_WIDTH = 2 * GLA_VALUE_WIDTH + GLA_KEY_WIDTH
N_ATTN_COL_TILES = ATTN_PROJ_WIDTH // INPROJ_COLS
N_GLA_COL_TILES = GLA_PROJ_WIDTH // INPROJ_COLS
GATE_PAD = LANES

_NT = (((1,), (1,)), ((), ()))


def _params(semantics):
    return pltpu.CompilerParams(dimension_semantics=semantics, vmem_limit_bytes=VMEM_LIMIT_BYTES)


def _bdot(a, b):
    return jnp.dot(a, b, preferred_element_type=F32)


def _inproj_kernel(x_ref, g_ref, w_ref, wkt_ref, wz_ref, wzt_ref, cos_ref, sin_ref,
                   pa_ref, pg_ref, kt_ref, z_ref, zt_ref, n_ref):
    j = pl.program_id(1)
    rows = x_ref.shape[0]
    q_scale = HEAD_DIM ** -0.5
    heads_per_tile = INPROJ_COLS // HEAD_DIM

    @pl.when(j == 0)
    def _():
        x = x_ref[...]
        inv = lax.rsqrt(jnp.mean(x * x, axis=-1, keepdims=True) + EPS)
        n = (x * inv * g_ref[...]).astype(BF16)
        n_ref[...] = n
        kt_ref[...] = lax.dot_general(wkt_ref[...], n, _NT, preferred_element_type=F32).astype(BF16)
        z_ref[...] = _bdot(n, wz_ref[...])
        zt_ref[...] = lax.dot_general(wzt_ref[...], n, _NT, preferred_element_type=F32)

    acc = _bdot(n_ref[...], w_ref[...])

    def rope_store(scale):
        cos = cos_ref[...]
        sin = sin_ref[...]
        lane = lax.broadcasted_iota(jnp.int32, (rows, HEAD_DIM), 1)
        for h in range(heads_per_tile):
            sl = slice(h * HEAD_DIM, (h + 1) * HEAD_DIM)
            t = acc[:, sl]
            partner = jnp.where(lane < ROPE_HALF, pltpu.roll(t, HEAD_DIM - ROPE_HALF, 1),
                                pltpu.roll(t, ROPE_HALF, 1))
            r = t * cos + partner * sin
            if scale is not None:
                r = r * scale
            pa_ref[:, sl] = r.astype(BF16)

    n_q_tiles = ATTN_WIDTH // INPROJ_COLS

    @pl.when(j < n_q_tiles)
    def _():
        rope_store(q_scale)

    @pl.when((j >= n_q_tiles) & (j < 2 * n_q_tiles))
    def _():
        rope_store(None)

    @pl.when((j >= 2 * n_q_tiles) & (j < N_ATTN_COL_TILES))
    def _():
        pa_ref[...] = acc.astype(BF16)

    @pl.when((j >= N_ATTN_COL_TILES) & (j < N_ATTN_COL_TILES + N_GLA_COL_TILES - 1))
    def _():
        pg_ref[...] = acc.astype(BF16)

    @pl.when(j == N_ATTN_COL_TILES + N_GLA_COL_TILES - 1)
    def _():
        pg_ref[...] = (acc * q_scale).astype(BF16)


def _inproj(x2, g1, w_main, w_kt, w_z, w_zt, cos_t, sin_t, seq):
    t_rows = x2.shape[0]
    tm = min(INPROJ_ROWS, seq)
    n_col = N_ATTN_COL_TILES + N_GLA_COL_TILES
    tiles_per_seq = seq // tm
    return pl.pallas_call(
        _inproj_kernel,
        name="inproj",
        grid=(t_rows // tm, n_col),
        in_specs=[
            pl.BlockSpec((tm, D_MODEL), lambda i, j: (i, 0)),
            pl.BlockSpec((1, D_MODEL), lambda i, j: (0, 0)),
            pl.BlockSpec((D_MODEL, INPROJ_COLS), lambda i, j: (0, j)),
            pl.BlockSpec((GLA_KEY_WIDTH, D_MODEL), lambda i, j: (0, 0)),
            pl.BlockSpec((D_MODEL, GATE_PAD), lambda i, j: (0, 0)),
            pl.BlockSpec((2 * GATE_RANK, D_MODEL), lambda i, j: (0, 0)),
            pl.BlockSpec((tm, HEAD_DIM), lambda i, j: (i % tiles_per_seq, 0)),
            pl.BlockSpec((tm, HEAD_DIM), lambda i, j: (i % tiles_per_seq, 0)),
        ],
        out_specs=[
            pl.BlockSpec((tm, INPROJ_COLS), lambda i, j: (i, jnp.minimum(j, N_ATTN_COL_TILES - 1))),
            pl.BlockSpec((tm, INPROJ_COLS), lambda i, j: (i, jnp.maximum(j - N_ATTN_COL_TILES, 0))),
            pl.BlockSpec((GLA_KEY_WIDTH, tm), lambda i, j: (0, i)),
            pl.BlockSpec((tm, GATE_PAD), lambda i, j: (i, 0)),
            pl.BlockSpec((2 * GATE_RANK, tm), lambda i, j: (0, i)),
        ],
        out_shape=[
            jax.ShapeDtypeStruct((t_rows, ATTN_PROJ_WIDTH), BF16),
            jax.ShapeDtypeStruct((t_rows, GLA_PROJ_WIDTH), BF16),
            jax.ShapeDtypeStruct((GLA_KEY_WIDTH, t_rows), BF16),
            jax.ShapeDtypeStruct((t_rows, GATE_PAD), F32),
            jax.ShapeDtypeStruct((2 * GATE_RANK, t_rows), F32),
        ],
        scratch_shapes=[pltpu.VMEM((tm, D_MODEL), BF16)],
        compiler_params=_params(("parallel", "arbitrary")),
    )(x2, g1, w_main, w_kt, w_z, w_zt, cos_t, sin_t)


def _attn_kernel(q_ref, kp_ref, kc_ref, kn_ref, vp_ref, vc_ref, vn_ref, o_ref, lse_ref, kwin, vwin, *, length):
    t = pl.program_id(2)
    tq = q_ref.shape[1]
    win = ATTN_QBLK + 2 * N_SIDE

    kwin[0:N_SIDE] = kp_ref[0]
    kwin[N_SIDE:N_SIDE + tq] = kc_ref[0]
    kwin[N_SIDE + tq:] = kn_ref[0]
    vwin[0:N_SIDE] = vp_ref[0]
    vwin[N_SIDE:N_SIDE + tq] = vc_ref[0]
    vwin[N_SIDE + tq:] = vn_ref[0]

    row = lax.broadcasted_iota(jnp.int32, (ATTN_QBLK, win), 0)
    col = lax.broadcasted_iota(jnp.int32, (ATTN_QBLK, win), 1)
    band = (col >= row) & (col <= row + 2 * N_SIDE)
    lane = lax.broadcasted_iota(jnp.int32, (ATTN_QBLK, LANES), 1)

    for jb in range(tq // ATTN_QBLK):
        r0 = jb * ATTN_QBLK
        first_key = t * tq + r0 - N_SIDE
        valid = band & (col >= -first_key) & (col < length - first_key)
        lse_tile = jnp.zeros((ATTN_QBLK, LANES), F32)
        for h in range(ATTN_HEADS):
            hs = slice(h * HEAD_DIM, (h + 1) * HEAD_DIM)
            q = q_ref[0, r0:r0 + ATTN_QBLK, hs]
            k = kwin[r0:r0 + win, hs]
            v = vwin[r0:r0 + win, hs]
            s = lax.dot_general(q, k, _NT, preferred_element_type=F32)
            s = jnp.where(valid, s, -1e30)
            m = jnp.max(s, axis=-1, keepdims=True)
            p = jnp.exp(s - m)
            den = jnp.sum(p, axis=-1, keepdims=True)
            o = _bdot(p.astype(BF16), v) / den
            o_ref[0, r0:r0 + ATTN_QBLK, hs] = o.astype(o_ref.dtype)
            lse_tile = jnp.where(lane == h, m + jnp.log(den), lse_tile)
        lse_ref[0, r0:r0 + ATTN_QBLK, :] = lse_tile


def _attention_branch(proj_a, batch, seq, dil):
    length = seq // dil
    tq = min(ATTN_ROWS, length)
    halo_per_tile = tq // N_SIDE
    n_halo_blocks = length // N_SIDE
    view = proj_a.reshape(batch, length, dil * ATTN_PROJ_WIDTH)
    n_parts = ATTN_PROJ_WIDTH // ATTN_WIDTH

    def cur(part):
        return pl.BlockSpec((1, tq, ATTN_WIDTH), lambda b, r, t: (b, t, r * n_parts + part))

    def prev(part):
        return pl.BlockSpec((1, N_SIDE, ATTN_WIDTH),
                            lambda b, r, t: (b, jnp.maximum(t * halo_per_tile - 1, 0), r * n_parts + part))

    def nxt(part):
        return pl.BlockSpec((1, N_SIDE, ATTN_WIDTH),
                            lambda b, r, t: (b, jnp.minimum((t + 1) * halo_per_tile, n_halo_blocks - 1),
                                             r * n_parts + part))

    o, lse = pl.pallas_call(
        functools.partial(_attn_kernel, length=length),
        name=f"attn_d{dil}",
        grid=(batch, dil, length // tq),
        in_specs=[cur(0), prev(1), cur(1), nxt(1), prev(2), cur(2), nxt(2)],
        out_specs=[
            pl.BlockSpec((1, tq, ATTN_WIDTH), lambda b, r, t: (b, t, r)),
            pl.BlockSpec((1, tq, LANES), lambda b, r, t: (b, t, r)),
        ],
        out_shape=[
            jax.ShapeDtypeStruct((batch, length, dil * ATTN_WIDTH), BF16),
            jax.ShapeDtypeStruct((batch, length, dil * LANES), F32),
        ],
        scratch_shapes=[pltpu.VMEM((tq + 2 * N_SIDE, ATTN_WIDTH), BF16),
                        pltpu.VMEM((tq + 2 * N_SIDE, ATTN_WIDTH), BF16)],
        compiler_params=_params(("parallel", "parallel", "parallel")),
    )(view, view, view, view, view, view, view)
    return o.reshape(batch * seq, ATTN_WIDTH), lse.reshape(batch * seq, LANES)


def _split3(a):
    hi = a.astype(BF16)
    r1 = a - hi.astype(F32)
    mid = r1.astype(BF16)
    lo = (r1 - mid.astype(F32)).astype(BF16)
    return hi, mid, lo


def _dot_f32(a, b):
    a_hi = a.astype(BF16)
    a_lo = (a - a_hi.astype(F32)).astype(BF16)
    b_hi = b.astype(BF16)
    b_lo = (b - b_hi.astype(F32)).astype(BF16)
    return _bdot(a_hi, b_hi) + (_bdot(a_hi, b_lo) + _bdot(a_lo, b_hi))


def _log_sigmoid(x):
    return jnp.minimum(x, 0.0) - jnp.log1p(jnp.exp(-jnp.abs(x)))


def _gla_direction(q_ref, kt_ref, v_ref, z_ref, zt_ref, up_ref, upt_ref, bias_ref, biast_ref, o_ref, state_ref,
                   forward):
    ts = q_ref.shape[0]
    c = GLA_CHUNK
    r = lax.broadcasted_iota(jnp.int32, (GLA_PAIR, GLA_PAIR), 0)
    cc = lax.broadcasted_iota(jnp.int32, (GLA_PAIR, GLA_PAIR), 1)
    same = jnp.where(r < c, 0, 1) == jnp.where(cc < c, 0, 1)
    lower = same & (r >= cc)
    upper = same & (r <= cc)
    if forward:
        left, right, causal = lower, upper, lower
        ref_i, last_i = c // 2 - 1, c - 1
        cross = (r >= c) & (cc < c)
        first_is_low = True
    else:
        left, right, causal = upper, lower, upper
        ref_i, last_i = c // 2, 0
        cross = (r < c) & (cc >= c)
        first_is_low = False
    left_m = jnp.where(left, 1.0, 0.0).astype(BF16)
    right_m = jnp.where(right, 1.0, 0.0).astype(BF16)
    row_low = r < c
    col_low = cc < c

    inv_norm = 1.0 / GATE_NORMALIZER
    g_all = _log_sigmoid(_dot_f32(z_ref[...], up_ref[...]) + bias_ref[...]) * inv_norm
    gt_all = _log_sigmoid(_dot_f32(upt_ref[...], zt_ref[...]) + biast_ref[...]) * inv_norm

    state = state_ref[...]
    n_pairs = ts // GLA_PAIR
    order = range(n_pairs) if forward else range(n_pairs - 1, -1, -1)
    for s in order:
        rows = slice(s * GLA_PAIR, (s + 1) * GLA_PAIR)
        g = g_all[rows]
        gt = gt_all[:, rows]
        g3 = _split3(g)
        b = _bdot(left_m, g3[0]) + (_bdot(left_m, g3[1]) + _bdot(left_m, g3[2]))
        gt3 = _split3(gt)
        bt = _bdot(gt3[0], right_m) + (_bdot(gt3[1], right_m) + _bdot(gt3[2], right_m))

        def rowpick(a, i):
            return jnp.where(row_low, a[i:i + 1, :], a[c + i:c + i + 1, :])

        def colpick(a, i):
            return jnp.where(col_low, a[:, i:i + 1], a[:, c + i:c + i + 1])

        q = q_ref[rows, :].astype(F32)
        kt = kt_ref[:, rows].astype(F32)
        v = v_ref[rows, :]
        qe = (q * jnp.exp(b - rowpick(b, ref_i))).astype(BF16)
        q_in = q * jnp.exp(b)
        ket = (kt * jnp.exp(colpick(bt, ref_i) - bt)).astype(BF16)
        kst = kt * jnp.exp(colpick(bt, last_i) - bt)

        a_diag = _bdot(qe, ket)
        a_cross = _bdot(q_in.astype(BF16), kst.astype(BF16))
        a = jnp.where(causal, a_diag, jnp.where(cross, a_cross, 0.0))
        o_intra = _bdot(a.astype(BF16), v)

        i_first = last_i if first_is_low else c + last_i
        i_second = c + last_i if first_is_low else last_i
        dec_first_row = jnp.exp(b[i_first:i_first + 1, :])
        dec_second_col = jnp.exp(bt[:, i_second:i_second + 1])
        dec_total_col = jnp.exp(bt[:, i_first:i_first + 1] + bt[:, i_second:i_second + 1])
        second_rows = (r >= c) if first_is_low else row_low
        first_cols = col_low if first_is_low else (cc >= c)
        q_state = (q_in * jnp.where(second_rows, dec_first_row, 1.0)).astype(BF16)
        k_state = (kst * jnp.where(first_cols, dec_second_col, 1.0)).astype(BF16)

        o_ref[rows, :] = o_intra + _bdot(q_state, state.astype(BF16))
        state = dec_total_col * state + _bdot(k_state, v)
    state_ref[...] = state


def _gla_kernel(qf_ref, ktf_ref, vf_ref, zf_ref, ztf_ref, qb_ref, ktb_ref, vb_ref, zb_ref, ztb_ref,
                upf_ref, uptf_ref, bf_ref, btf_ref, upb_ref, uptb_ref, bb_ref, btb_ref,
                of_ref, ob_ref, sf_ref, sb_ref):
    @pl.when(pl.program_id(2) == 0)
    def _():
        sf_ref[...] = jnp.zeros_like(sf_ref)
        sb_ref[...] = jnp.zeros_like(sb_ref)

    _gla_direction(qf_ref, ktf_ref, vf_ref, zf_ref, ztf_ref, upf_ref, uptf_ref, bf_ref, btf_ref, of_ref, sf_ref, True)
    _gla_direction(qb_ref, ktb_ref, vb_ref, zb_ref, ztb_ref, upb_ref, uptb_ref, bb_ref, btb_ref, ob_ref, sb_ref, False)


def _gla(proj_g, kt, z, zt, gf_up, gf_b, gb_up, gb_b, batch, seq):
    t_rows = batch * seq
    ts = min(GLA_ROWS, seq)
    n_s = seq // ts
    q_col0 = 2 * GLA_VALUE_WIDTH // GLA_DK

    zero = jnp.zeros((GATE_RANK, GLA_KEY_WIDTH), F32)
    pad_rows = jnp.zeros((GATE_PAD - 2 * GATE_RANK, GLA_KEY_WIDTH), F32)
    upf = jnp.concatenate([gf_up, zero, pad_rows], axis=0)
    upb = jnp.concatenate([zero, gb_up, pad_rows], axis=0)
    uptf = jnp.concatenate([gf_up, zero], axis=0).T
    uptb = jnp.concatenate([zero, gb_up], axis=0).T
    bf = gf_b.reshape(1, GLA_KEY_WIDTH)
    bb = gb_b.reshape(1, GLA_KEY_WIDTH)
    btf = gf_b.reshape(GLA_KEY_WIDTH, 1)
    btb = gb_b.reshape(GLA_KEY_WIDTH, 1)

    def tile_specs(tile):
        return [
            pl.BlockSpec((ts, GLA_DK), lambda b, h, i: (tile(b, i), q_col0 + h)),
            pl.BlockSpec((GLA_DK, ts), lambda b, h, i: (h, tile(b, i))),
            pl.BlockSpec((ts, GLA_DV), lambda b, h, i: (tile(b, i), h)),
            pl.BlockSpec((ts, GATE_PAD), lambda b, h, i: (tile(b, i), 0)),
            pl.BlockSpec((2 * GATE_RANK, ts), lambda b, h, i: (0, tile(b, i))),
        ]

    fwd_tile = lambda b, i: b * n_s + i
    bwd_tile = lambda b, i: b * n_s + (n_s - 1 - i)
    gate_specs = [
        pl.BlockSpec((GATE_PAD, GLA_DK), lambda b, h, i: (0, h)),
        pl.BlockSpec((GLA_DK, 2 * GATE_RANK), lambda b, h, i: (h, 0)),
        pl.BlockSpec((1, GLA_DK), lambda b, h, i: (0, h)),
        pl.BlockSpec((GLA_DK, 1), lambda b, h, i: (h, 0)),
    ]
    return pl.pallas_call(
        _gla_kernel,
        name="gla",
        grid=(batch, GLA_HEADS, n_s),
        in_specs=tile_specs(fwd_tile) + tile_specs(bwd_tile) + gate_specs + gate_specs,
        out_specs=[
            pl.BlockSpec((ts, GLA_DV), lambda b, h, i: (fwd_tile(b, i), h)),
            pl.BlockSpec((ts, GLA_DV), lambda b, h, i: (bwd_tile(b, i), h)),
        ],
        out_shape=[jax.ShapeDtypeStruct((t_rows, GLA_VALUE_WIDTH), F32)] * 2,
        scratch_shapes=[pltpu.VMEM((GLA_DK, GLA_DV), F32), pltpu.VMEM((GLA_DK, GLA_DV), F32)],
        compiler_params=_params(("parallel", "parallel", "arbitrary")),
    )(proj_g, kt, proj_g, z, zt, proj_g, kt, proj_g, z, zt, upf, uptf, bf, btf, upb, uptb, bb, btb)


def _outproj_kernel(o1_ref, o2_ref, o3_ref, l1_ref, l2_ref, l3_ref, of_ref, ob_ref, gr_ref, x_ref,
                    ag_ref, gg_ref, w_ref, h_ref, ao_ref, cat_ref):
    rows = x_ref.shape[0]
    l1, l2, l3 = l1_ref[...], l2_ref[...], l3_ref[...]
    m = jnp.maximum(jnp.maximum(l1, l2), l3)
    e1, e2, e3 = jnp.exp(l1 - m), jnp.exp(l2 - m), jnp.exp(l3 - m)
    inv = 1.0 / (e1 + e2 + e3)
    w1, w2, w3 = e1 * inv, e2 * inv, e3 * inv

    ssq = jnp.zeros((rows, 1), F32)
    for h in range(ATTN_HEADS):
        hs = slice(h * HEAD_DIM, (h + 1) * HEAD_DIM)
        ao = (w1[:, h:h + 1] * o1_ref[:, hs].astype(F32) + w2[:, h:h + 1] * o2_ref[:, hs].astype(F32)
              + w3[:, h:h + 1] * o3_ref[:, hs].astype(F32))
        ssq = ssq + jnp.sum(ao * ao, axis=-1, keepdims=True)
        ao_ref[:, hs] = ao
    inv_rms = lax.rsqrt(ssq * (1.0 / ATTN_WIDTH) + EPS)
    cat_ref[:, 0:ATTN_WIDTH] = (ao_ref[...] * inv_rms * ag_ref[...]).astype(BF16)

    for h in range(GLA_HEADS):
        hs = slice(h * GLA_DV, (h + 1) * GLA_DV)
        o = of_ref[:, hs] + ob_ref[:, hs]
        y = o * lax.rsqrt(jnp.mean(o * o, axis=-1, keepdims=True) + EPS) * gg_ref[...]
        gate = gr_ref[:, hs].astype(F32)
        y = y * (gate / (1.0 + jnp.exp(-gate)))
        cat_ref[:, ATTN_WIDTH + h * GLA_DV:ATTN_WIDTH + (h + 1) * GLA_DV] = y.astype(BF16)

    h_ref[...] = x_ref[...] + _bdot(cat_ref[...], w_ref[...])


def _outproj(branches, o_f, o_b, proj_g, x2, attn_g, gla_g, w_out):
    t_rows = x2.shape[0]
    tm = OUTPROJ_ROWS
    row = lambda width: pl.BlockSpec((tm, width), lambda i: (i, 0))
    (o1, l1), (o2, l2), (o3, l3) = branches
    return pl.pallas_call(
        _outproj_kernel,
        name="outproj",
        grid=(t_rows // tm,),
        in_specs=[
            row(ATTN_WIDTH), row(ATTN_WIDTH), row(ATTN_WIDTH), row(LANES), row(LANES), row(LANES),
            row(GLA_VALUE_WIDTH), row(GLA_VALUE_WIDTH),
            pl.BlockSpec((tm, GLA_VALUE_WIDTH), lambda i: (i, 1)),
            row(D_MODEL),
            pl.BlockSpec((1, ATTN_WIDTH), lambda i: (0, 0)),
            pl.BlockSpec((1, GLA_DV), lambda i: (0, 0)),
            pl.BlockSpec((D_MODEL, D_MODEL), lambda i: (0, 0)),
        ],
        out_specs=row(D_MODEL),
        out_shape=jax.ShapeDtypeStruct((t_rows, D_MODEL), F32),
        scratch_shapes=[pltpu.VMEM((tm, ATTN_WIDTH), F32), pltpu.VMEM((tm, D_MODEL), BF16)],
        compiler_params=_params(("parallel",)),
    )(o1, o2, o3, l1, l2, l3, o_f, o_b, proj_g, x2, attn_g, gla_g, w_out)


def _ffn_kernel(h_ref, hp_ref, hn_ref, g2_ref, wg_ref, wu_ref, cw_ref, cb_ref, wd_ref, gf_ref,
                out_ref, n_ref, gate_ref, acc_ref, *, tiles_per_seq, final_norm):
    i = pl.program_id(0)
    j = pl.program_id(1)
    tm = h_ref.shape[0]
    halo = FFN_HALO

    def norm(x):
        return (x * lax.rsqrt(jnp.mean(x * x, axis=-1, keepdims=True) + EPS) * g2_ref[...]).astype(BF16)

    @pl.when(j == 0)
    def _():
        n_ref[0:halo] = norm(hp_ref[...])
        n_ref[halo:halo + tm] = norm(h_ref[...])
        n_ref[halo + tm:] = norm(hn_ref[...])
        acc_ref[...] = jnp.zeros_like(acc_ref)

    gate_ref[...] = _bdot(n_ref[...], wg_ref[...])
    row = lax.broadcasted_iota(jnp.int32, (tm, 1), 0)
    seq_tile = i % tiles_per_seq
    g_prev = jnp.where((row == 0) & (seq_tile == 0), 0.0, gate_ref[halo - 1:halo - 1 + tm, :])
    g_next = jnp.where((row == tm - 1) & (seq_tile == tiles_per_seq - 1), 0.0, gate_ref[halo + 1:halo + 1 + tm, :])
    g_mid = gate_ref[halo:halo + tm, :]
    cw = cw_ref[...]
    conv = g_prev * cw[0:1, :] + g_mid * cw[1:2, :] + g_next * cw[2:3, :] + cb_ref[...]
    up = _bdot(n_ref[halo:halo + tm, :], wu_ref[...])
    act = (conv / (1.0 + jnp.exp(-conv))) * up
    acc_ref[...] += _bdot(act.astype(BF16), wd_ref[...])

    @pl.when(j == pl.num_programs(1) - 1)
    def _():
        y = h_ref[...] + acc_ref[...]
        if final_norm:
            y = y * lax.rsqrt(jnp.mean(y * y, axis=-1, keepdims=True) + EPS) * gf_ref[...]
        out_ref[...] = y


def _ffn(h2, g2, wg, wu, conv_w, conv_b, wd, gfinal, seq, final_norm):
    t_rows = h2.shape[0]
    tm = min(FFN_ROWS, seq)
    tf = FFN_COLS
    halo_blocks_per_tile = tm // FFN_HALO
    n_halo_blocks = t_rows // FFN_HALO
    return pl.pallas_call(
        functools.partial(_ffn_kernel, tiles_per_seq=seq // tm, final_norm=final_norm),
        name="ffn",
        grid=(t_rows // tm, D_FF // tf),
        in_specs=[
            pl.BlockSpec((tm, D_MODEL), lambda i, j: (i, 0)),
            pl.BlockSpec((FFN_HALO, D_MODEL), lambda i, j: (jnp.maximum(i * halo_blocks_per_tile - 1, 0), 0)),
            pl.BlockSpec((FFN_HALO, D_MODEL),
                         lambda i, j: (jnp.minimum((i + 1) * halo_blocks_per_tile, n_halo_blocks - 1), 0)),
            pl.BlockSpec((1, D_MODEL), lambda i, j: (0, 0)),
            pl.BlockSpec((D_MODEL, tf), lambda i, j: (0, j)),
            pl.BlockSpec((D_MODEL, tf), lambda i, j: (0, j)),
            pl.BlockSpec((CONV_WIDTH, tf), lambda i, j: (0, j)),
            pl.BlockSpec((1, tf), lambda i, j: (0, j)),
            pl.BlockSpec((tf, D_MODEL), lambda i, j: (j, 0)),
            pl.BlockSpec((1, D_MODEL), lambda i, j: (0, 0)),
        ],
        out_specs=pl.BlockSpec((tm, D_MODEL), lambda i, j: (i, 0)),
        out_shape=jax.ShapeDtypeStruct((t_rows, D_MODEL), F32),
        scratch_shapes=[pltpu.VMEM((tm + 2 * FFN_HALO, D_MODEL), BF16),
                        pltpu.VMEM((tm + 2 * FFN_HALO, tf), F32),
                        pltpu.VMEM((tm, D_MODEL), F32)],
        compiler_params=_params(("parallel", "arbitrary")),
    )(h2, h2, h2, g2, wg, wu, conv_w, conv_b, wd, gfinal)


def _rope_tables(seq):
    pos = jnp.arange(seq, dtype=F32)
    inv_freq = ROPE_THETA ** (-jnp.arange(0, ROPE_DIM, 2, dtype=F32) / ROPE_DIM)
    ang = pos[:, None] * inv_freq[None, :]
    cos, sin = jnp.cos(ang), jnp.sin(ang)
    rest = HEAD_DIM - ROPE_DIM
    cos_t = jnp.concatenate([cos, cos, jnp.ones((seq, rest), F32)], axis=1)
    sin_t = jnp.concatenate([-sin, sin, jnp.zeros((seq, rest), F32)], axis=1)
    return cos_t, sin_t


def kernel(x, norm1_g, w_in, gf_up, gf_b, gb_up, gb_b, gla_norm_g, attn_norm_g, w_out, norm2_g, w_gate, w_up,
           conv_w, conv_b, w_down, final_norm_g):
    batch, seq, d_model = x.shape
    depth = norm1_g.shape[0]
    assert d_model == D_MODEL and w_in.shape[-1] == 3 * ATTN_WIDTH + 2 * GLA_KEY_WIDTH + 2 * GLA_VALUE_WIDTH + 2 * GATE_RANK
    assert all((window // 2) // dil == N_SIDE for window, dil in DILATED_PATTERNS)
    assert seq % (max(d for _, d in DILATED_PATTERNS) * N_SIDE) == 0 and seq % GLA_ROWS == 0

    cos_t, sin_t = _rope_tables(seq)
    aw, kw, vw = ATTN_WIDTH, GLA_KEY_WIDTH, GLA_VALUE_WIDTH
    c_gq, c_gk, c_gv, c_gr, c_z = 3 * aw, 3 * aw + kw, 3 * aw + 2 * kw, 3 * aw + 2 * kw + vw, 3 * aw + 2 * kw + 2 * vw

    h2 = x.reshape(batch * seq, D_MODEL)
    for l in range(depth):
        wi = w_in[l]
        w_main = jnp.concatenate([wi[:, :c_gq], wi[:, c_gv:c_gr], wi[:, c_gr:c_z], wi[:, c_gq:c_gk]], axis=1).astype(BF16)
        w_kt = wi[:, c_gk:c_gv].T.astype(BF16)
        w_zcols = wi[:, c_z:]
        w_z = jnp.pad(w_zcols, ((0, 0), (0, GATE_PAD - 2 * GATE_RANK))).astype(BF16)
        w_zt = w_zcols.T.astype(BF16)

        proj_a, proj_g, kt, z, zt = _inproj(h2, norm1_g[l].reshape(1, D_MODEL), w_main, w_kt, w_z, w_zt,
                                            cos_t, sin_t, seq)
        branches = [_attention_branch(proj_a, batch, seq, dil) for _, dil in DILATED_PATTERNS]
        o_f, o_b = _gla(proj_g, kt, z, zt, gf_up[l], gf_b[l], gb_up[l], gb_b[l], batch, seq)
        h2 = _outproj(branches, o_f, o_b, proj_g, h2, attn_norm_g[l].reshape(1, ATTN_WIDTH),
                      gla_norm_g[l].reshape(1, GLA_DV), w_out[l].astype(BF16))
        h2 = _ffn(h2, norm2_g[l].reshape(1, D_MODEL), w_gate[l].astype(BF16), w_up[l].astype(BF16), conv_w[l],
                  conv_b[l].reshape(1, D_FF), w_down[l].astype(BF16), final_norm_g.reshape(1, D_MODEL), seq,
                  final_norm=(l == depth - 1))
    return h2.reshape(batch, seq, D_MODEL)
```

```python
import functools

import jax
import jax.numpy as jnp
from jax import lax
from jax.experimental import pallas as pl
from jax.experimental.pallas import tpu as pltpu

F32 = jnp.float32
BF16 = jnp.bfloat16

D_MODEL = 2048
ATTN_WIDTH = 1024
HEAD_DIM = 128
ATTN_HEADS = ATTN_WIDTH // HEAD_DIM
DILATED_PATTERNS = ((128, 1), (512, 4), (2048, 16))
DILATIONS = tuple(d for _, d in DILATED_PATTERNS)
N_SIDE = 64
ROPE_THETA = 500000.0
ROPE_DIM = HEAD_DIM // 4
ROPE_HALF = ROPE_DIM // 2
GLA_HEADS = 4
GLA_DK = 128
GLA_DV = 256
GLA_KEY_WIDTH = GLA_HEADS * GLA_DK
GLA_VALUE_WIDTH = GLA_HEADS * GLA_DV
GATE_RANK = 16
GATE_NORMALIZER = 16.0
GLA_CHUNK = 64
D_FF = 5632
CONV_WIDTH = 3
EPS = 1e-6

LANES = 128
BF16_SUBLANES = 16
VMEM_LIMIT_BYTES = 56 * 1024 * 1024

INPROJ_ROWS = 1024
INPROJ_COLS = 512
ATTN_TILE = 2048
ATTN_QBLK = 128
ATTN_HEAD_GROUP = 4
ATTN_GROUP_WIDTH = ATTN_HEAD_GROUP * HEAD_DIM
ATTN_GROUPS = ATTN_HEADS // ATTN_HEAD_GROUP
GLA_ROWS = 512
GLA_PAIR = 2 * GLA_CHUNK
OUTPROJ_ROWS = 256
FFN_ROWS = 1024
FFN_COLS = 512
FFN_HALO = BF16_SUBLANES

ATTN_PROJ_WIDTH = 3 * ATTN_WIDTH
GLA_PROJ_WIDTH = 2 * GLA_VALUE_WIDTH + GLA_KEY_WIDTH
N_ATTN_COL_TILES = ATTN_PROJ_WIDTH // INPROJ_COLS
N_GLA_COL_TILES = GLA_PROJ_WIDTH // INPROJ_COLS
GATE_PAD = LANES
ROPE_PARTNER_SHIFT = HEAD_DIM // 2

_NT = (((1,), (1,)), ((), ()))


def _params(semantics):
    return pltpu.CompilerParams(dimension_semantics=semantics, vmem_limit_bytes=VMEM_LIMIT_BYTES)


def _bdot(a, b):
    return jnp.dot(a, b, preferred_element_type=F32)


def _inproj_kernel(x_ref, g_ref, w_ref, wkt_ref, wz_ref, cos_ref, sin_ref,
                   pa_ref, pa4_ref, pa16_ref, pg_ref, kt_ref, z_ref, zt_ref, n_ref, rs_ref):
    j = pl.program_id(1)
    rows = x_ref.shape[0]
    q_scale = HEAD_DIM ** -0.5
    heads_per_tile = INPROJ_COLS // HEAD_DIM

    @pl.when(j == 0)
    def _():
        x = x_ref[...]
        inv = lax.rsqrt(jnp.mean(x * x, axis=-1, keepdims=True) + EPS)
        n = (x * inv * g_ref[...]).astype(BF16)
        n_ref[...] = n
        kt_ref[...] = lax.dot_general(wkt_ref[...], n, _NT, preferred_element_type=F32).astype(BF16)
        z = _bdot(n, wz_ref[...])
        z_ref[...] = z
        zt_ref[...] = z.T[0:2 * GATE_RANK, :]

    acc = _bdot(n_ref[...], w_ref[...])

    def attn_store(rotary, scale):
        for h in range(heads_per_tile):
            sl = slice(h * HEAD_DIM, (h + 1) * HEAD_DIM)
            r = acc[:, sl]
            if rotary:
                r = r * cos_ref[...] + pltpu.roll(r, ROPE_PARTNER_SHIFT, 1) * sin_ref[...]
            if scale is not None:
                r = r * scale
            pa_ref[:, sl] = r.astype(BF16)
            rs_ref[h] = r
        for h in range(heads_per_tile):
            sl = slice(h * HEAD_DIM, (h + 1) * HEAD_DIM)
            for dil, out in ((4, pa4_ref), (16, pa16_ref)):
                for res in range(dil):
                    out[0, res, :, sl] = rs_ref[h, pl.ds(res, rows // dil, stride=dil), :].astype(BF16)

    n_q_tiles = ATTN_WIDTH // INPROJ_COLS

    @pl.when(j < n_q_tiles)
    def _():
        attn_store(True, q_scale)

    @pl.when((j >= n_q_tiles) & (j < 2 * n_q_tiles))
    def _():
        attn_store(True, None)

    @pl.when((j >= 2 * n_q_tiles) & (j < N_ATTN_COL_TILES))
    def _():
        attn_store(False, None)

    @pl.when((j >= N_ATTN_COL_TILES) & (j < N_ATTN_COL_TILES + N_GLA_COL_TILES - 1))
    def _():
        pg_ref[...] = acc.astype(BF16)

    @pl.when(j == N_ATTN_COL_TILES + N_GLA_COL_TILES - 1)
    def _():
        pg_ref[...] = (acc * q_scale).astype(BF16)


def _inproj(x2, g1, w_main, w_kt, w_z, cos_t, sin_t, batch, seq):
    t_rows = x2.shape[0]
    tm = min(INPROJ_ROWS, seq)
    n_col = N_ATTN_COL_TILES + N_GLA_COL_TILES
    tps = seq // tm
    attn_col = lambda j: jnp.minimum(j, N_ATTN_COL_TILES - 1)

    def residue_spec(dil):
        return pl.BlockSpec((1, dil, tm // dil, INPROJ_COLS), lambda i, j: (i // tps, 0, i % tps, attn_col(j)))

    return pl.pallas_call(
        _inproj_kernel,
        name="inproj",
        grid=(t_rows // tm, n_col),
        in_specs=[
            pl.BlockSpec((tm, D_MODEL), lambda i, j: (i, 0)),
            pl.BlockSpec((1, D_MODEL), lambda i, j: (0, 0)),
            pl.BlockSpec((D_MODEL, INPROJ_COLS), lambda i, j: (0, j)),
            pl.BlockSpec((GLA_KEY_WIDTH, D_MODEL), lambda i, j: (0, 0)),
            pl.BlockSpec((D_MODEL, GATE_PAD), lambda i, j: (0, 0)),
            pl.BlockSpec((tm, HEAD_DIM), lambda i, j: (i % tps, 0)),
            pl.BlockSpec((tm, HEAD_DIM), lambda i, j: (i % tps, 0)),
        ],
        out_specs=[
            pl.BlockSpec((tm, INPROJ_COLS), lambda i, j: (i, attn_col(j))),
            residue_spec(4),
            residue_spec(16),
            pl.BlockSpec((tm, INPROJ_COLS), lambda i, j: (i, jnp.maximum(j - N_ATTN_COL_TILES, 0))),
            pl.BlockSpec((GLA_KEY_WIDTH, tm), lambda i, j: (0, i)),
            pl.BlockSpec((tm, GATE_PAD), lambda i, j: (i, 0)),
            pl.BlockSpec((2 * GATE_RANK, tm), lambda i, j: (0, i)),
        ],
        out_shape=[
            jax.ShapeDtypeStruct((t_rows, ATTN_PROJ_WIDTH), BF16),
            jax.ShapeDtypeStruct((batch, 4, seq // 4, ATTN_PROJ_WIDTH), BF16),
            jax.ShapeDtypeStruct((batch, 16, seq // 16, ATTN_PROJ_WIDTH), BF16),
            jax.ShapeDtypeStruct((t_rows, GLA_PROJ_WIDTH), BF16),
            jax.ShapeDtypeStruct((GLA_KEY_WIDTH, t_rows), BF16),
            jax.ShapeDtypeStruct((t_rows, GATE_PAD), F32),
            jax.ShapeDtypeStruct((2 * GATE_RANK, t_rows), F32),
        ],
        scratch_shapes=[pltpu.VMEM((tm, D_MODEL), BF16),
                        pltpu.VMEM((INPROJ_COLS // HEAD_DIM, tm, HEAD_DIM), F32)],
        compiler_params=_params(("parallel", "arbitrary")),
    )(x2, g1, w_main, w_kt, w_z, cos_t, sin_t)


def _attn_kernel(q_ref, kp_ref, kc_ref, kn_ref, vp_ref, vc_ref, vn_ref, o_ref, lse_ref, kwin, vwin, osc,
                 *, length, dil):
    t = pl.program_id(2)
    tq = q_ref.shape[2]
    win = ATTN_QBLK + 2 * N_SIDE
    n_blk = tq // ATTN_QBLK

    kwin[:, 0:N_SIDE, :] = kp_ref[0]
    kwin[:, N_SIDE:N_SIDE + tq, :] = kc_ref[0]
    kwin[:, N_SIDE + tq:, :] = kn_ref[0]
    vwin[:, 0:N_SIDE, :] = vp_ref[0]
    vwin[:, N_SIDE:N_SIDE + tq, :] = vc_ref[0]
    vwin[:, N_SIDE + tq:, :] = vn_ref[0]

    row = lax.broadcasted_iota(jnp.int32, (ATTN_QBLK, win), 0)
    col = lax.broadcasted_iota(jnp.int32, (ATTN_QBLK, win), 1)
    band = (col >= row) & (col <= row + 2 * N_SIDE)
    lane = lax.broadcasted_iota(jnp.int32, (ATTN_QBLK, LANES), 1)

    for res, blk in [(a, b) for a in range(dil) for b in range(n_blk)]:
        r0 = blk * ATTN_QBLK
        first_key = t * tq + r0 - N_SIDE
        valid = band & (col >= -first_key) & (col < length - first_key)
        if dil == 1:
            dst = pl.ds(r0, ATTN_QBLK)
        else:
            dst = pl.ds(r0 * dil + res, ATTN_QBLK, stride=dil)
        lse_tile = jnp.zeros((ATTN_QBLK, LANES), F32)
        for h in range(ATTN_HEAD_GROUP):
            hs = slice(h * HEAD_DIM, (h + 1) * HEAD_DIM)
            q = q_ref[0, res, pl.ds(r0, ATTN_QBLK), hs]
            k = kwin[res, pl.ds(r0, win), hs]
            v = vwin[res, pl.ds(r0, win), hs]
            s = lax.dot_general(q, k, _NT, preferred_element_type=F32)
            s = jnp.where(valid, s, -1e30)
            m = jnp.max(s, axis=-1, keepdims=True)
            p = jnp.exp(s - m)
            den = jnp.sum(p, axis=-1, keepdims=True)
            osc[h, dst, :] = _bdot(p.astype(BF16), v) / den
            lse_tile = jnp.where(lane == h, m + jnp.log(den), lse_tile)
        lse_ref[dst, :] = lse_tile

    for h in range(ATTN_HEAD_GROUP):
        o_ref[:, h * HEAD_DIM:(h + 1) * HEAD_DIM] = osc[h].astype(o_ref.dtype)


def _attention_branch(proj_res, batch, seq, dil):
    length = seq // dil
    tn = min(ATTN_TILE, seq)
    tq = tn // dil
    n_tiles = seq // tn
    halo_per_tile = tq // N_SIDE
    n_halo_blocks = length // N_SIDE
    gw = ATTN_GROUP_WIDTH
    part_blocks = ATTN_WIDTH // gw

    def cur(part):
        return pl.BlockSpec((1, dil, tq, gw), lambda b, g, t: (b, 0, t, part * part_blocks + g))

    def prev(part):
        return pl.BlockSpec((1, dil, N_SIDE, gw),
                            lambda b, g, t: (b, 0, jnp.maximum(t * halo_per_tile - 1, 0), part * part_blocks + g))

    def nxt(part):
        return pl.BlockSpec((1, dil, N_SIDE, gw),
                            lambda b, g, t: (b, 0, jnp.minimum((t + 1) * halo_per_tile, n_halo_blocks - 1),
                                             part * part_blocks + g))

    return pl.pallas_call(
        functools.partial(_attn_kernel, length=length, dil=dil),
        name=f"attn_d{dil}",
        grid=(batch, ATTN_GROUPS, n_tiles),
        in_specs=[cur(0), prev(1), cur(1), nxt(1), prev(2), cur(2), nxt(2)],
        out_specs=[
            pl.BlockSpec((tn, gw), lambda b, g, t: (b * n_tiles + t, g)),
            pl.BlockSpec((tn, LANES), lambda b, g, t: (b * n_tiles + t, g)),
        ],
        out_shape=[
            jax.ShapeDtypeStruct((batch * seq, ATTN_WIDTH), BF16),
            jax.ShapeDtypeStruct((batch * seq, ATTN_GROUPS * LANES), F32),
        ],
        scratch_shapes=[pltpu.VMEM((dil, tq + 2 * N_SIDE, gw), BF16),
                        pltpu.VMEM((dil, tq + 2 * N_SIDE, gw), BF16),
                        pltpu.VMEM((ATTN_HEAD_GROUP, tn, HEAD_DIM), F32)],
        compiler_params=_params(("parallel", "parallel", "parallel")),
    )(proj_res, proj_res, proj_res, proj_res, proj_res, proj_res, proj_res)


def _split2(a):
    hi = a.astype(BF16)
    lo = (a - hi.astype(F32)).astype(BF16)
    return hi, lo


def _dot_f32(a, b):
    a_hi, a_lo = _split2(a)
    b_hi, b_lo = _split2(b)
    return _bdot(a_hi, b_hi) + (_bdot(a_hi, b_lo) + _bdot(a_lo, b_hi))


def _log_sigmoid(x):
    return jnp.minimum(x, 0.0) - jnp.log1p(jnp.exp(-jnp.abs(x)))


class _Direction:
    def __init__(self, refs, forward):
        (self.q_ref, self.kt_ref, self.v_ref, z_ref, zt_ref, up_ref, upt_ref, bias_ref, biast_ref,
         self.o_ref, self.state_ref) = refs
        c = GLA_CHUNK
        r = lax.broadcasted_iota(jnp.int32, (GLA_PAIR, GLA_PAIR), 0)
        cc = lax.broadcasted_iota(jnp.int32, (GLA_PAIR, GLA_PAIR), 1)
        same = jnp.where(r < c, 0, 1) == jnp.where(cc < c, 0, 1)
        lower = same & (r >= cc)
        upper = same & (r <= cc)
        self.row_low = r < c
        self.col_low = cc < c
        if forward:
            left, right, self.causal = lower, upper, lower
            self.ref_i, self.last_i = c // 2 - 1, c - 1
            self.cross = (r >= c) & (cc < c)
            self.i_first, self.i_second = self.last_i, c + self.last_i
            self.second_rows, self.first_cols = r >= c, cc < c
        else:
            left, right, self.causal = upper, lower, upper
            self.ref_i, self.last_i = c // 2, 0
            self.cross = (r < c) & (cc >= c)
            self.i_first, self.i_second = c + self.last_i, self.last_i
            self.second_rows, self.first_cols = r < c, cc >= c
        self.left_m = jnp.where(left, 1.0, 0.0).astype(BF16)
        self.right_m = jnp.where(right, 1.0, 0.0).astype(BF16)
        inv_norm = 1.0 / GATE_NORMALIZER
        self.g = _log_sigmoid(_dot_f32(z_ref[...], up_ref[...]) + bias_ref[...]) * inv_norm
        self.gt = _log_sigmoid(_dot_f32(upt_ref[...], zt_ref[...]) + biast_ref[...]) * inv_norm


def _gla_kernel(qf_ref, ktf_ref, vf_ref, zf_ref, ztf_ref, qb_ref, ktb_ref, vb_ref, zb_ref, ztb_ref,
                upf_ref, uptf_ref, bf_ref, btf_ref, upb_ref, uptb_ref, bb_ref, btb_ref,
                of_ref, ob_ref, sf_ref, sb_ref):
    @pl.when(pl.program_id(2) == 0)
    def _():
        sf_ref[...] = jnp.zeros_like(sf_ref)
        sb_ref[...] = jnp.zeros_like(sb_ref)

    c = GLA_CHUNK
    ts = qf_ref.shape[0]
    n_pairs = ts // GLA_PAIR
    fwd = _Direction((qf_ref, ktf_ref, vf_ref, zf_ref, ztf_ref, upf_ref, uptf_ref, bf_ref, btf_ref, of_ref, sf_ref),
                     True)
    bwd = _Direction((qb_ref, ktb_ref, vb_ref, zb_ref, ztb_ref, upb_ref, uptb_ref, bb_ref, btb_ref, ob_ref, sb_ref),
                     False)
    jobs = []
    for s in range(n_pairs):
        jobs.append((fwd, slice(s * GLA_PAIR, (s + 1) * GLA_PAIR)))
        jobs.append((bwd, slice((n_pairs - 1 - s) * GLA_PAIR, (n_pairs - s) * GLA_PAIR)))

    cums = []
    for d, rows in jobs:
        g_hi, g_lo = _split2(d.g[rows])
        gt_hi, gt_lo = _split2(d.gt[:, rows])
        b = _bdot(d.left_m, g_hi) + _bdot(d.left_m, g_lo)
        bt = _bdot(gt_hi, d.right_m) + _bdot(gt_lo, d.right_m)
        cums.append((b, bt))

    ops = []
    for (d, rows), (b, bt) in zip(jobs, cums):
        def rowpick(a, i):
            return jnp.where(d.row_low, a[i:i + 1, :], a[c + i:c + i + 1, :])

        def colpick(a, i):
            return jnp.where(d.col_low, a[:, i:i + 1], a[:, c + i:c + i + 1])

        q = d.q_ref[rows, :].astype(F32)
        kt = d.kt_ref[:, rows].astype(F32)
        qe = (q * jnp.exp(b - rowpick(b, d.ref_i))).astype(BF16)
        q_in = q * jnp.exp(b)
        ket = (kt * jnp.exp(colpick(bt, d.ref_i) - bt)).astype(BF16)
        kst = kt * jnp.exp(colpick(bt, d.last_i) - bt)
        dec_first_row = jnp.exp(b[d.i_first:d.i_first + 1, :])
        dec_second_col = jnp.exp(bt[:, d.i_second:d.i_second + 1])
        dec_total_col = jnp.exp(bt[:, d.i_first:d.i_first + 1] + bt[:, d.i_second:d.i_second + 1])
        q_state = (q_in * jnp.where(d.second_rows, dec_first_row, 1.0)).astype(BF16)
        k_state = (kst * jnp.where(d.first_cols, dec_second_col, 1.0)).astype(BF16)
        ops.append((qe, ket, q_in.astype(BF16), kst.astype(BF16), q_state, k_state, dec_total_col))

    scores = []
    for qe, ket, q_in, kst, _, _, _ in ops:
        scores.append((_bdot(qe, ket), _bdot(q_in, kst)))

    locals_ = []
    for (d, rows), (a_diag, a_cross), op in zip(jobs, scores, ops):
        a = jnp.where(d.causal, a_diag, jnp.where(d.cross, a_cross, 0.0)).astype(BF16)
        v = d.v_ref[rows, :]
        locals_.append((_bdot(a, v), _bdot(op[5], v)))

    state = {id(fwd): sf_ref[...], id(bwd): sb_ref[...]}
    for (d, rows), (o_intra, upd), op in zip(jobs, locals_, ops):
        s = state[id(d)]
        d.o_ref[rows, :] = o_intra + _bdot(op[4], s.astype(BF16))
        state[id(d)] = op[6] * s + upd
    sf_ref[...] = state[id(fwd)]
    sb_ref[...] = state[id(bwd)]


def _gla(proj_g, kt, z, zt, gf_up, gf_b, gb_up, gb_b, batch, seq):
    t_rows = batch * seq
    ts = min(GLA_ROWS, seq)
    n_s = seq // ts
    q_col0 = 2 * GLA_VALUE_WIDTH // GLA_DK

    zero = jnp.zeros((GATE_RANK, GLA_KEY_WIDTH), F32)
    pad_rows = jnp.zeros((GATE_PAD - 2 * GATE_RANK, GLA_KEY_WIDTH), F32)
    upf = jnp.concatenate([gf_up, zero, pad_rows], axis=0)
    upb = jnp.concatenate([zero, gb_up, pad_rows], axis=0)
    uptf = jnp.concatenate([gf_up, zero], axis=0).T
    uptb = jnp.concatenate([zero, gb_up], axis=0).T
    bf = gf_b.reshape(1, GLA_KEY_WIDTH)
    bb = gb_b.reshape(1, GLA_KEY_WIDTH)
    btf = gf_b.reshape(GLA_KEY_WIDTH, 1)
    btb = gb_b.reshape(GLA_KEY_WIDTH, 1)

    def tile_specs(tile):
        return [
            pl.BlockSpec((ts, GLA_DK), lambda b, h, i: (tile(b, i), q_col0 + h)),
            pl.BlockSpec((GLA_DK, ts), lambda b, h, i: (h, tile(b, i))),
            pl.BlockSpec((ts, GLA_DV), lambda b, h, i: (tile(b, i), h)),
            pl.BlockSpec((ts, GATE_PAD), lambda b, h, i: (tile(b, i), 0)),
            pl.BlockSpec((2 * GATE_RANK, ts), lambda b, h, i: (0, tile(b, i))),
        ]

    fwd_tile = lambda b, i: b * n_s + i
    bwd_tile = lambda b, i: b * n_s + (n_s - 1 - i)
    gate_specs = [
        pl.BlockSpec((GATE_PAD, GLA_DK), lambda b, h, i: (0, h)),
        pl.BlockSpec((GLA_DK, 2 * GATE_RANK), lambda b, h, i: (h, 0)),
        pl.BlockSpec((1, GLA_DK), lambda b, h, i: (0, h)),
        pl.BlockSpec((GLA_DK, 1), lambda b, h, i: (h, 0)),
    ]
    return pl.pallas_call(
        _gla_kernel,
        name="gla",
        grid=(batch, GLA_HEADS, n_s),
        in_specs=tile_specs(fwd_tile) + tile_specs(bwd_tile) + gate_specs + gate_specs,
        out_specs=[
            pl.BlockSpec((ts, GLA_DV), lambda b, h, i: (fwd_tile(b, i), h)),
            pl.BlockSpec((ts, GLA_DV), lambda b, h, i: (bwd_tile(b, i), h)),
        ],
        out_shape=[jax.ShapeDtypeStruct((t_rows, GLA_VALUE_WIDTH), F32)] * 2,
        scratch_shapes=[pltpu.VMEM((GLA_DK, GLA_DV), F32), pltpu.VMEM((GLA_DK, GLA_DV), F32)],
        compiler_params=_params(("parallel", "parallel", "arbitrary")),
    )(proj_g, kt, proj_g, z, zt, proj_g, kt, proj_g, z, zt, upf, uptf, bf, btf, upb, uptb, bb, btb)


def _outproj_kernel(o1_ref, o2_ref, o3_ref, l1_ref, l2_ref, l3_ref, of_ref, ob_ref, gr_ref, x_ref,
                    ag_ref, gg_ref, w_ref, h_ref, ao_ref, cat_ref):
    rows = x_ref.shape[0]
    l1, l2, l3 = l1_ref[...], l2_ref[...], l3_ref[...]
    m = jnp.maximum(jnp.maximum(l1, l2), l3)
    e1, e2, e3 = jnp.exp(l1 - m), jnp.exp(l2 - m), jnp.exp(l3 - m)
    inv = 1.0 / (e1 + e2 + e3)
    w1, w2, w3 = e1 * inv, e2 * inv, e3 * inv

    ssq = jnp.zeros((rows, 1), F32)
    for h in range(ATTN_HEADS):
        hs = slice(h * HEAD_DIM, (h + 1) * HEAD_DIM)
        lc = (h // ATTN_HEAD_GROUP) * LANES + h % ATTN_HEAD_GROUP
        ao = (w1[:, lc:lc + 1] * o1_ref[:, hs].astype(F32) + w2[:, lc:lc + 1] * o2_ref[:, hs].astype(F32)
              + w3[:, lc:lc + 1] * o3_ref[:, hs].astype(F32))
        ssq = ssq + jnp.sum(ao * ao, axis=-1, keepdims=True)
        ao_ref[:, hs] = ao
    inv_rms = lax.rsqrt(ssq * (1.0 / ATTN_WIDTH) + EPS)
    cat_ref[:, 0:ATTN_WIDTH] = (ao_ref[...] * inv_rms * ag_ref[...]).astype(BF16)

    for h in range(GLA_HEADS):
        hs = slice(h * GLA_DV, (h + 1) * GLA_DV)
        o = of_ref[:, hs] + ob_ref[:, hs]
        y = o * lax.rsqrt(jnp.mean(o * o, axis=-1, keepdims=True) + EPS) * gg_ref[...]
        gate = gr_ref[:, hs].astype(F32)
        y = y * (gate / (1.0 + jnp.exp(-gate)))
        cat_ref[:, ATTN_WIDTH + h * GLA_DV:ATTN_WIDTH + (h + 1) * GLA_DV] = y.astype(BF16)

    h_ref[...] = x_ref[...] + _bdot(cat_ref[...], w_ref[...])


def _outproj(branches, o_f, o_b, proj_g, x2, attn_g, gla_g, w_out):
    t_rows = x2.shape[0]
    tm = OUTPROJ_ROWS
    row = lambda width: pl.BlockSpec((tm, width), lambda i: (i, 0))
    (o1, l1), (o2, l2), (o3, l3) = branches
    lse_w = ATTN_GROUPS * LANES
    return pl.pallas_call(
        _outproj_kernel,
        name="outproj",
        grid=(t_rows // tm,),
        in_specs=[
            row(ATTN_WIDTH), row(ATTN_WIDTH), row(ATTN_WIDTH), row(lse_w), row(lse_w), row(lse_w),
            row(GLA_VALUE_WIDTH), row(GLA_VALUE_WIDTH),
            pl.BlockSpec((tm, GLA_VALUE_WIDTH), lambda i: (i, 1)),
            row(D_MODEL),
            pl.BlockSpec((1, ATTN_WIDTH), lambda i: (0, 0)),
            pl.BlockSpec((1, GLA_DV), lambda i: (0, 0)),
            pl.BlockSpec((D_MODEL, D_MODEL), lambda i: (0, 0)),
        ],
        out_specs=row(D_MODEL),
        out_shape=jax.ShapeDtypeStruct((t_rows, D_MODEL), F32),
        scratch_shapes=[pltpu.VMEM((tm, ATTN_WIDTH), F32), pltpu.VMEM((tm, D_MODEL), BF16)],
        compiler_params=_params(("parallel",)),
    )(o1, o2, o3, l1, l2, l3, o_f, o_b, proj_g, x2, attn_g, gla_g, w_out)


def _ffn_kernel(h_ref, hp_ref, hn_ref, g2_ref, wg_ref, wu_ref, cw_ref, cb_ref, wd_ref, gf_ref,
                out_ref, n_ref, gate_ref, *, tiles_per_seq, final_norm):
    i = pl.program_id(0)
    j = pl.program_id(1)
    tm = h_ref.shape[0]
    halo = FFN_HALO

    def norm(x):
        return (x * lax.rsqrt(jnp.mean(x * x, axis=-1, keepdims=True) + EPS) * g2_ref[...]).astype(BF16)

    @pl.when(j == 0)
    def _():
        n_ref[0:halo] = norm(hp_ref[...])
        n_ref[halo:halo + tm] = norm(h_ref[...])
        n_ref[halo + tm:] = norm(hn_ref[...])
        out_ref[...] = h_ref[...]

    gate_ref[...] = _bdot(n_ref[...], wg_ref[...])
    row = lax.broadcasted_iota(jnp.int32, (tm, 1), 0)
    seq_tile = i % tiles_per_seq
    g_prev = jnp.where((row == 0) & (seq_tile == 0), 0.0, gate_ref[halo - 1:halo - 1 + tm, :])
    g_next = jnp.where((row == tm - 1) & (seq_tile == tiles_per_seq - 1), 0.0, gate_ref[halo + 1:halo + 1 + tm, :])
    g_mid = gate_ref[halo:halo + tm, :]
    cw = cw_ref[...]
    conv = g_prev * cw[0:1, :] + g_mid * cw[1:2, :] + g_next * cw[2:3, :] + cb_ref[...]
    up = _bdot(n_ref[halo:halo + tm, :], wu_ref[...])
    act = (conv / (1.0 + jnp.exp(-conv))) * up
    out_ref[...] += _bdot(act.astype(BF16), wd_ref[...])

    if final_norm:
        @pl.when(j == pl.num_programs(1) - 1)
        def _():
            y = out_ref[...]
            out_ref[...] = y * lax.rsqrt(jnp.mean(y * y, axis=-1, keepdims=True) + EPS) * gf_ref[...]


def _ffn(h2, g2, wg, wu, conv_w, conv_b, wd, gfinal, seq, final_norm):
    t_rows = h2.shape[0]
    tm = min(FFN_ROWS, seq)
    tf = FFN_COLS
    halo_blocks_per_tile = tm // FFN_HALO
    n_halo_blocks = t_rows // FFN_HALO
    return pl.pallas_call(
        functools.partial(_ffn_kernel, tiles_per_seq=seq // tm, final_norm=final_norm),
        name="ffn",
        grid=(t_rows // tm, D_FF // tf),
        in_specs=[
            pl.BlockSpec((tm, D_MODEL), lambda i, j: (i, 0), pipeline_mode=pl.Buffered(1)),
            pl.BlockSpec((FFN_HALO, D_MODEL), lambda i, j: (jnp.maximum(i * halo_blocks_per_tile - 1, 0), 0)),
            pl.BlockSpec((FFN_HALO, D_MODEL),
                         lambda i, j: (jnp.minimum((i + 1) * halo_blocks_per_tile, n_halo_blocks - 1), 0)),
            pl.BlockSpec((1, D_MODEL), lambda i, j: (0, 0)),
            pl.BlockSpec((D_MODEL, tf), lambda i, j: (0, j)),
            pl.BlockSpec((D_MODEL, tf), lambda i, j: (0, j)),
            pl.BlockSpec((CONV_WIDTH, tf), lambda i, j: (0, j)),
            pl.BlockSpec((1, tf), lambda i, j: (0, j)),
            pl.BlockSpec((tf, D_MODEL), lambda i, j: (j, 0)),
            pl.BlockSpec((1, D_MODEL), lambda i, j: (0, 0)),
        ],
        out_specs=pl.BlockSpec((tm, D_MODEL), lambda i, j: (i, 0)),
        out_shape=jax.ShapeDtypeStruct((t_rows, D_MODEL), F32),
        scratch_shapes=[pltpu.VMEM((tm + 2 * FFN_HALO, D_MODEL), BF16),
                        pltpu.VMEM((tm + 2 * FFN_HALO, tf), F32)],
        compiler_params=_params(("parallel", "arbitrary")),
    )(h2, h2, h2, g2, wg, wu, conv_w, conv_b, wd, gfinal)


def _rope_tables(seq):
    pos = jnp.arange(seq, dtype=F32)
    inv_freq = ROPE_THETA ** (-jnp.arange(0, ROPE_DIM, 2, dtype=F32) / ROPE_DIM)
    ang = pos[:, None] * inv_freq[None, :]
    cos, sin = jnp.cos(ang), jnp.sin(ang)
    gap = ROPE_PARTNER_SHIFT - ROPE_HALF
    ones, zeros = jnp.ones((seq, gap), F32), jnp.zeros((seq, gap), F32)
    cos_t = jnp.concatenate([cos, ones, cos, ones], axis=1)
    sin_t = jnp.concatenate([-sin, zeros, sin, zeros], axis=1)
    return cos_t, sin_t


def _rotary_head_layout(w):
    d_in, width = w.shape
    w = w.reshape(d_in, width // HEAD_DIM, HEAD_DIM)
    split = ROPE_DIM + ROPE_PARTNER_SHIFT - ROPE_HALF
    w = jnp.concatenate([w[..., :ROPE_HALF], w[..., ROPE_DIM:split], w[..., ROPE_HALF:ROPE_DIM], w[..., split:]],
                        axis=-1)
    return w.reshape(d_in, width)


def kernel(x, norm1_g, w_in, gf_up, gf_b, gb_up, gb_b, gla_norm_g, attn_norm_g, w_out, norm2_g, w_gate, w_up,
           conv_w, conv_b, w_down, final_norm_g):
    batch, seq, d_model = x.shape
    depth = norm1_g.shape[0]
    aw, kw, vw = ATTN_WIDTH, GLA_KEY_WIDTH, GLA_VALUE_WIDTH
    assert d_model == D_MODEL and w_in.shape[-1] == 3 * aw + 2 * kw + 2 * vw + 2 * GATE_RANK
    assert all((window // 2) // dil == N_SIDE for window, dil in DILATED_PATTERNS) and DILATIONS == (1, 4, 16)
    assert seq % ATTN_TILE == 0 or seq == ATTN_TILE // 2

    cos_t, sin_t = _rope_tables(seq)
    c_gq, c_gk, c_gv, c_gr, c_z = 3 * aw, 3 * aw + kw, 3 * aw + 2 * kw, 3 * aw + 2 * kw + vw, 3 * aw + 2 * kw + 2 * vw

    h2 = x.reshape(batch * seq, D_MODEL)
    for l in range(depth):
        wi = w_in[l]
        w_main = jnp.concatenate([_rotary_head_layout(wi[:, :2 * aw]), wi[:, 2 * aw:c_gq], wi[:, c_gv:c_gr],
                                  wi[:, c_gr:c_z], wi[:, c_gq:c_gk]], axis=1).astype(BF16)
        w_kt = wi[:, c_gk:c_gv].T.astype(BF16)
        w_z = jnp.pad(wi[:, c_z:], ((0, 0), (0, GATE_PAD - 2 * GATE_RANK))).astype(BF16)

        proj_a, proj_a4, proj_a16, proj_g, kt, z, zt = _inproj(
            h2, norm1_g[l].reshape(1, D_MODEL), w_main, w_kt, w_z, cos_t, sin_t, batch, seq)
        by_residue = {1: proj_a.reshape(batch, 1, seq, ATTN_PROJ_WIDTH), 4: proj_a4, 16: proj_a16}
        branches = [_attention_branch(by_residue[dil], batch, seq, dil) for dil in DILATIONS]
        o_f, o_b = _gla(proj_g, kt, z, zt, gf_up[l], gf_b[l], gb_up[l], gb_b[l], batch, seq)
        h2 = _outproj(branches, o_f, o_b, proj_g, h2, attn_norm_g[l].reshape(1, ATTN_WIDTH),
                      gla_norm_g[l].reshape(1, GLA_DV), w_out[l].astype(BF16))
        h2 = _ffn(h2, norm2_g[l].reshape(1, D_MODEL), w_gate[l].astype(BF16), w_up[l].astype(BF16), conv_w[l],
                  conv_b[l].reshape(1, D_FF), w_down[l].astype(BF16), final_norm_g.reshape(1, D_MODEL), seq,
                  final_norm=(l == depth - 1))
    return h2.reshape(batch, seq, D_MODEL)
```

```python
import functools

import jax
import jax.numpy as jnp
import numpy as np
from jax import lax
from jax.experimental import pallas as pl
from jax.experimental.pallas import tpu as pltpu

F32 = jnp.float32
BF16 = jnp.bfloat16

D_MODEL = 2048
ATTN_WIDTH = 1024
HEAD_DIM = 128
ATTN_HEADS = ATTN_WIDTH // HEAD_DIM
DILATED_PATTERNS = ((128, 1), (512, 4), (2048, 16))
DILATIONS = tuple(d for _, d in DILATED_PATTERNS)
N_SIDE = 64
ROPE_THETA = 500000.0
ROPE_DIM = HEAD_DIM // 4
ROPE_HALF = ROPE_DIM // 2
GLA_HEADS = 4
GLA_DK = 128
GLA_DV = 256
GLA_KEY_WIDTH = GLA_HEADS * GLA_DK
GLA_VALUE_WIDTH = GLA_HEADS * GLA_DV
GATE_RANK = 16
GATE_NORMALIZER = 16.0
GLA_CHUNK = 64
D_FF = 5632
CONV_WIDTH = 3
EPS = 1e-6

LANES = 128
BF16_SUBLANES = 16
VMEM_LIMIT_BYTES = 56 * 1024 * 1024

INPROJ_ROWS = 1024
INPROJ_COLS = 512
INPROJ_NORM_CHUNKS = 4
ATTN_TILE = 2048
ATTN_QBLK = 128
ATTN_HEAD_GROUP = 4
ATTN_GROUP_WIDTH = ATTN_HEAD_GROUP * HEAD_DIM
ATTN_GROUPS = ATTN_HEADS // ATTN_HEAD_GROUP
GLA_ROWS = 512
GLA_PAIR = 2 * GLA_CHUNK
OUTPROJ_ROWS = 256
FFN_ROWS = 1024
FFN_COLS = 512
FFN_HALO = BF16_SUBLANES

ATTN_PROJ_WIDTH = 3 * ATTN_WIDTH
PROJ_WIDTH = ATTN_PROJ_WIDTH + 2 * GLA_VALUE_WIDTH + GLA_KEY_WIDTH
N_COL_TILES = PROJ_WIDTH // INPROJ_COLS
N_ATTN_COL_TILES = ATTN_PROJ_WIDTH // INPROJ_COLS
RESIDUE_WIDTH = ATTN_PROJ_WIDTH + INPROJ_COLS
GLA_V_COL = ATTN_PROJ_WIDTH
GLA_GATE_COL = GLA_V_COL + GLA_VALUE_WIDTH
GLA_Q_COL = GLA_GATE_COL + GLA_VALUE_WIDTH
ROPE_PARTNER_SHIFT = HEAD_DIM // 2

_NT = (((1,), (1,)), ((), ()))


def _params(semantics):
    return pltpu.CompilerParams(dimension_semantics=semantics, vmem_limit_bytes=VMEM_LIMIT_BYTES)


def _bdot(a, b):
    return jnp.dot(a, b, preferred_element_type=F32)


def _inproj_kernel(x_ref, g_ref, w_ref, wkz_ref, cos_ref, sin_ref,
                   p_ref, p4_ref, p16_ref, kt_ref, z_ref, zt_ref, n_ref, rs_a, rs_b, g4_ref):
    i = pl.program_id(0)
    j = pl.program_id(1)
    rows = x_ref.shape[0]
    heads_per_tile = INPROJ_COLS // HEAD_DIM

    @pl.when((i == 0) & (j == 0))
    def _():
        rs_b[...] = jnp.zeros_like(rs_b)

    @pl.when(j == 0)
    def _():
        chunk = rows // INPROJ_NORM_CHUNKS
        for c in range(INPROJ_NORM_CHUNKS):
            rs = slice(c * chunk, (c + 1) * chunk)
            x = x_ref[rs, :]
            inv = lax.rsqrt(jnp.mean(x * x, axis=-1, keepdims=True) + EPS)
            n = (x * inv * g_ref[...]).astype(BF16)
            n_ref[rs, :] = n
            kz = lax.dot_general(wkz_ref[...], n, _NT, preferred_element_type=F32)
            kt_ref[:, rs] = kz[0:GLA_KEY_WIDTH, :].astype(BF16)
            zt = kz[GLA_KEY_WIDTH:, :]
            zt_ref[:, rs] = zt
            z_ref[rs, :] = zt.T

    done = jnp.maximum(j - 1, 0)
    scaled = (done < ATTN_WIDTH // INPROJ_COLS) | (done == N_COL_TILES - 1)
    scale = jnp.where(scaled, HEAD_DIM ** -0.5, 1.0).astype(F32)

    def step(cur, prev):
        cos = cos_ref[...]
        sin = sin_ref[...]
        for h in range(heads_per_tile):
            sl = slice(h * HEAD_DIM, (h + 1) * HEAD_DIM)
            t = prev[h]
            r = (t * cos + pltpu.roll(t, ROPE_PARTNER_SHIFT, 1) * sin) * scale
            p_ref[:, sl] = r.astype(BF16)
            prev[h] = r
        per4 = rows // 4
        per16 = rows // 16
        for h in range(heads_per_tile):
            sl = slice(h * HEAD_DIM, (h + 1) * HEAD_DIM)
            for res4 in range(4):
                grp = prev[h, pl.ds(res4, per4, stride=4), :]
                p4_ref[0, res4, :, sl] = grp.astype(BF16)
                g4_ref[h, res4 * per4:(res4 + 1) * per4, :] = grp
            for res4 in range(4):
                for a in range(4):
                    p16_ref[0, 4 * a + res4, :, sl] = g4_ref[h, pl.ds(res4 * per4 + a, per16, stride=4), :].astype(BF16)
        acc = _bdot(n_ref[...], w_ref[...])
        for h in range(heads_per_tile):
            cur[h] = acc[:, h * HEAD_DIM:(h + 1) * HEAD_DIM]

    @pl.when(j % 2 == 0)
    def _():
        step(rs_a, rs_b)

    @pl.when(j % 2 == 1)
    def _():
        step(rs_b, rs_a)


def _inproj(x2, g1, w_main, w_kz, cos_t, sin_t, batch, seq):
    t_rows = x2.shape[0]
    tm = min(INPROJ_ROWS, seq)
    tps = seq // tm
    done = lambda j: jnp.maximum(j - 1, 0)
    n_rotary_tiles = 2 * ATTN_WIDTH // INPROJ_COLS
    table_spec = pl.BlockSpec((tm, HEAD_DIM), lambda i, j: (jnp.where(done(j) < n_rotary_tiles, i % tps, tps), 0))

    def residue_spec(dil):
        return pl.BlockSpec((1, dil, tm // dil, INPROJ_COLS),
                            lambda i, j: (i // tps, 0, i % tps, jnp.minimum(done(j), N_ATTN_COL_TILES)))

    slab = pltpu.VMEM((INPROJ_COLS // HEAD_DIM, tm, HEAD_DIM), F32)
    return pl.pallas_call(
        _inproj_kernel,
        name="inproj",
        grid=(t_rows // tm, N_COL_TILES + 1),
        in_specs=[
            pl.BlockSpec((tm, D_MODEL), lambda i, j: (i, 0)),
            pl.BlockSpec((1, D_MODEL), lambda i, j: (0, 0)),
            pl.BlockSpec((D_MODEL, INPROJ_COLS), lambda i, j: (0, jnp.minimum(j, N_COL_TILES - 1))),
            pl.BlockSpec((GLA_KEY_WIDTH + 2 * GATE_RANK, D_MODEL), lambda i, j: (0, 0)),
            table_spec,
            table_spec,
        ],
        out_specs=[
            pl.BlockSpec((tm, INPROJ_COLS), lambda i, j: (i, done(j))),
            residue_spec(4),
            residue_spec(16),
            pl.BlockSpec((GLA_KEY_WIDTH, tm), lambda i, j: (0, i)),
            pl.BlockSpec((tm, 2 * GATE_RANK), lambda i, j: (i, 0)),
            pl.BlockSpec((2 * GATE_RANK, tm), lambda i, j: (0, i)),
        ],
        out_shape=[
            jax.ShapeDtypeStruct((t_rows, PROJ_WIDTH), BF16),
            jax.ShapeDtypeStruct((batch, 4, seq // 4, RESIDUE_WIDTH), BF16),
            jax.ShapeDtypeStruct((batch, 16, seq // 16, RESIDUE_WIDTH), BF16),
            jax.ShapeDtypeStruct((GLA_KEY_WIDTH, t_rows), BF16),
            jax.ShapeDtypeStruct((t_rows, 2 * GATE_RANK), F32),
            jax.ShapeDtypeStruct((2 * GATE_RANK, t_rows), F32),
        ],
        scratch_shapes=[pltpu.VMEM((tm, D_MODEL), BF16), slab, slab, slab],
        compiler_params=_params(("arbitrary", "arbitrary")),
    )(x2, g1, w_main, w_kz, cos_t, sin_t)


def _attn_kernel(q_ref, kp_ref, kc_ref, kn_ref, vp_ref, vc_ref, vn_ref, o_ref, lse_ref, kwin, vwin, osc,
                 *, length, dil):
    t = pl.program_id(2)
    tq = q_ref.shape[2]
    win = ATTN_QBLK + 2 * N_SIDE
    n_blk = tq // ATTN_QBLK

    kwin[:, 0:N_SIDE, :] = kp_ref[0]
    kwin[:, N_SIDE:N_SIDE + tq, :] = kc_ref[0]
    kwin[:, N_SIDE + tq:, :] = kn_ref[0]
    vwin[:, 0:N_SIDE, :] = vp_ref[0]
    vwin[:, N_SIDE:N_SIDE + tq, :] = vc_ref[0]
    vwin[:, N_SIDE + tq:, :] = vn_ref[0]

    row = lax.broadcasted_iota(jnp.int32, (ATTN_QBLK, win), 0)
    col = lax.broadcasted_iota(jnp.int32, (ATTN_QBLK, win), 1)
    band = (col >= row) & (col <= row + 2 * N_SIDE)
    lane = lax.broadcasted_iota(jnp.int32, (ATTN_QBLK, LANES), 1)

    for res, blk in [(a, b) for a in range(dil) for b in range(n_blk)]:
        r0 = blk * ATTN_QBLK
        first_key = t * tq + r0 - N_SIDE
        valid = band & (col >= -first_key) & (col < length - first_key)
        if dil == 1:
            dst = pl.ds(r0, ATTN_QBLK)
        else:
            dst = pl.ds(r0 * dil + res, ATTN_QBLK, stride=dil)
        lse_tile = jnp.zeros((ATTN_QBLK, LANES), F32)
        for h in range(ATTN_HEAD_GROUP):
            hs = slice(h * HEAD_DIM, (h + 1) * HEAD_DIM)
            q = q_ref[0, res, pl.ds(r0, ATTN_QBLK), hs]
            k = kwin[res, pl.ds(r0, win), hs]
            v = vwin[res, pl.ds(r0, win), hs]
            s = lax.dot_general(q, k, _NT, preferred_element_type=F32)
            s = jnp.where(valid, s, -1e30)
            m = jnp.max(s, axis=-1, keepdims=True)
            p = jnp.exp(s - m)
            den = jnp.sum(p, axis=-1, keepdims=True)
            osc[h, dst, :] = _bdot(p.astype(BF16), v) / den
            lse_tile = jnp.where(lane == h, m + jnp.log(den), lse_tile)
        lse_ref[dst, :] = lse_tile

    for h in range(ATTN_HEAD_GROUP):
        o_ref[:, h * HEAD_DIM:(h + 1) * HEAD_DIM] = osc[h].astype(o_ref.dtype)


def _attention_branch(proj_res, batch, seq, dil):
    length = seq // dil
    tn = min(ATTN_TILE, seq)
    tq = tn // dil
    n_tiles = seq // tn
    halo_per_tile = tq // N_SIDE
    n_halo_blocks = length // N_SIDE
    gw = ATTN_GROUP_WIDTH
    part_blocks = ATTN_WIDTH // gw

    def cur(part):
        return pl.BlockSpec((1, dil, tq, gw), lambda b, g, t: (b, 0, t, part * part_blocks + g))

    def prev(part):
        return pl.BlockSpec((1, dil, N_SIDE, gw),
                            lambda b, g, t: (b, 0, jnp.maximum(t * halo_per_tile - 1, 0), part * part_blocks + g))

    def nxt(part):
        return pl.BlockSpec((1, dil, N_SIDE, gw),
                            lambda b, g, t: (b, 0, jnp.minimum((t + 1) * halo_per_tile, n_halo_blocks - 1),
                                             part * part_blocks + g))

    return pl.pallas_call(
        functools.partial(_attn_kernel, length=length, dil=dil),
        name=f"attn_d{dil}",
        grid=(batch, ATTN_GROUPS, n_tiles),
        in_specs=[cur(0), prev(1), cur(1), nxt(1), prev(2), cur(2), nxt(2)],
        out_specs=[
            pl.BlockSpec((tn, gw), lambda b, g, t: (b * n_tiles + t, g)),
            pl.BlockSpec((tn, LANES), lambda b, g, t: (b * n_tiles + t, g)),
        ],
        out_shape=[
            jax.ShapeDtypeStruct((batch * seq, ATTN_WIDTH), BF16),
            jax.ShapeDtypeStruct((batch * seq, ATTN_GROUPS * LANES), F32),
        ],
        scratch_shapes=[pltpu.VMEM((dil, tq + 2 * N_SIDE, gw), BF16),
                        pltpu.VMEM((dil, tq + 2 * N_SIDE, gw), BF16),
                        pltpu.VMEM((ATTN_HEAD_GROUP, tn, HEAD_DIM), F32)],
        compiler_params=_params(("parallel", "parallel", "parallel")),
    )(proj_res, proj_res, proj_res, proj_res, proj_res, proj_res, proj_res)


def _split2(a):
    hi = a.astype(BF16)
    lo = (a - hi.astype(F32)).astype(BF16)
    return hi, lo


def _dot_f32(a, b):
    a_hi, a_lo = _split2(a)
    b_hi, b_lo = _split2(b)
    return _bdot(a_hi, b_hi) + (_bdot(a_hi, b_lo) + _bdot(a_lo, b_hi))


def _log_sigmoid(x):
    return jnp.minimum(x, 0.0) - jnp.log(1.0 + jnp.exp(-jnp.abs(x)))


class _Direction:
    def __init__(self, refs, forward):
        (self.q_ref, self.kt_ref, self.v_ref, z_ref, zt_ref, up_ref, upt_ref, bias_ref, biast_ref,
         self.o_ref, self.state_ref) = refs
        c = GLA_CHUNK
        r = lax.broadcasted_iota(jnp.int32, (GLA_PAIR, GLA_PAIR), 0)
        cc = lax.broadcasted_iota(jnp.int32, (GLA_PAIR, GLA_PAIR), 1)
        same = jnp.where(r < c, 0, 1) == jnp.where(cc < c, 0, 1)
        lower = same & (r >= cc)
        upper = same & (r <= cc)
        self.row_low = r < c
        self.col_low = cc < c
        if forward:
            left, right, self.causal = lower, upper, lower
            self.ref_i, self.last_i = c // 2 - 1, c - 1
            self.cross = (r >= c) & (cc < c)
            self.i_first, self.i_second = self.last_i, c + self.last_i
            self.second_rows, self.first_cols = r >= c, cc < c
        else:
            left, right, self.causal = upper, lower, upper
            self.ref_i, self.last_i = c // 2, 0
            self.cross = (r < c) & (cc >= c)
            self.i_first, self.i_second = c + self.last_i, self.last_i
            self.second_rows, self.first_cols = r < c, cc >= c
        self.left_m = jnp.where(left, 1.0, 0.0).astype(BF16)
        self.right_m = jnp.where(right, 1.0, 0.0).astype(BF16)
        inv_norm = 1.0 / GATE_NORMALIZER
        self.g = _log_sigmoid(_dot_f32(z_ref[...], up_ref[...]) + bias_ref[...]) * inv_norm
        self.gt = _log_sigmoid(_dot_f32(upt_ref[...], zt_ref[...]) + biast_ref[...]) * inv_norm


def _gla_kernel(qf_ref, ktf_ref, vf_ref, zf_ref, ztf_ref, qb_ref, ktb_ref, vb_ref, zb_ref, ztb_ref,
                upf_ref, uptf_ref, bf_ref, btf_ref, upb_ref, uptb_ref, bb_ref, btb_ref,
                of_ref, ob_ref, sf_ref, sb_ref):
    @pl.when(pl.program_id(2) == 0)
    def _():
        sf_ref[...] = jnp.zeros_like(sf_ref)
        sb_ref[...] = jnp.zeros_like(sb_ref)

    c = GLA_CHUNK
    ts = qf_ref.shape[0]
    n_pairs = ts // GLA_PAIR
    fwd = _Direction((qf_ref, ktf_ref, vf_ref, zf_ref, ztf_ref, upf_ref, uptf_ref, bf_ref, btf_ref, of_ref, sf_ref),
                     True)
    bwd = _Direction((qb_ref, ktb_ref, vb_ref, zb_ref, ztb_ref, upb_ref, uptb_ref, bb_ref, btb_ref, ob_ref, sb_ref),
                     False)
    jobs = []
    for s in range(n_pairs):
        jobs.append((fwd, slice(s * GLA_PAIR, (s + 1) * GLA_PAIR)))
        jobs.append((bwd, slice((n_pairs - 1 - s) * GLA_PAIR, (n_pairs - s) * GLA_PAIR)))

    cums = []
    for d, rows in jobs:
        g_hi, g_lo = _split2(d.g[rows])
        gt_hi, gt_lo = _split2(d.gt[:, rows])
        b = _bdot(d.left_m, g_hi) + _bdot(d.left_m, g_lo)
        bt = _bdot(gt_hi, d.right_m) + _bdot(gt_lo, d.right_m)
        cums.append((b, bt))

    ops = []
    for (d, rows), (b, bt) in zip(jobs, cums):
        def rowpick(a, i):
            return jnp.where(d.row_low, a[i:i + 1, :], a[c + i:c + i + 1, :])

        def colpick(a, i):
            return jnp.where(d.col_low, a[:, i:i + 1], a[:, c + i:c + i + 1])

        q = d.q_ref[rows, :].astype(F32)
        kt = d.kt_ref[:, rows].astype(F32)
        qe = (q * jnp.exp(b - rowpick(b, d.ref_i))).astype(BF16)
        q_in = q * jnp.exp(b)
        ket = (kt * jnp.exp(colpick(bt, d.ref_i) - bt)).astype(BF16)
        kst = kt * jnp.exp(colpick(bt, d.last_i) - bt)
        dec_first_row = jnp.exp(b[d.i_first:d.i_first + 1, :])
        dec_second_col = jnp.exp(bt[:, d.i_second:d.i_second + 1])
        dec_total_col = jnp.exp(bt[:, d.i_first:d.i_first + 1] + bt[:, d.i_second:d.i_second + 1])
        q_state = (q_in * jnp.where(d.second_rows, dec_first_row, 1.0)).astype(BF16)
        k_state = (kst * jnp.where(d.first_cols, dec_second_col, 1.0)).astype(BF16)
        ops.append((qe, ket, q_in.astype(BF16), kst.astype(BF16), q_state, k_state, dec_total_col))

    scores = []
    for qe, ket, q_in, kst, _, _, _ in ops:
        scores.append((_bdot(qe, ket), _bdot(q_in, kst)))

    locals_ = []
    for (d, rows), (a_diag, a_cross), op in zip(jobs, scores, ops):
        a = jnp.where(d.causal, a_diag, jnp.where(d.cross, a_cross, 0.0)).astype(BF16)
        v = d.v_ref[rows, :]
        locals_.append((_bdot(a, v), _bdot(op[5], v)))

    state = {id(fwd): sf_ref[...], id(bwd): sb_ref[...]}
    for (d, rows), (o_intra, upd), op in zip(jobs, locals_, ops):
        s = state[id(d)]
        d.o_ref[rows, :] = o_intra + _bdot(op[4], s.astype(BF16))
        state[id(d)] = op[6] * s + upd
    sf_ref[...] = state[id(fwd)]
    sb_ref[...] = state[id(bwd)]


def _gla(proj, kt, z, zt, gf_up, gf_b, gb_up, gb_b, batch, seq):
    t_rows = batch * seq
    ts = min(GLA_ROWS, seq)
    n_s = seq // ts
    q_col0 = GLA_Q_COL // GLA_DK
    v_col0 = GLA_V_COL // GLA_DV

    zero = jnp.zeros((GATE_RANK, GLA_KEY_WIDTH), F32)
    upf = jnp.concatenate([gf_up, zero], axis=0)
    upb = jnp.concatenate([zero, gb_up], axis=0)
    uptf = upf.T
    uptb = upb.T
    bf = gf_b.reshape(1, GLA_KEY_WIDTH)
    bb = gb_b.reshape(1, GLA_KEY_WIDTH)
    btf = gf_b.reshape(GLA_KEY_WIDTH, 1)
    btb = gb_b.reshape(GLA_KEY_WIDTH, 1)

    def tile_specs(tile):
        return [
            pl.BlockSpec((ts, GLA_DK), lambda b, h, i: (tile(b, i), q_col0 + h)),
            pl.BlockSpec((GLA_DK, ts), lambda b, h, i: (h, tile(b, i))),
            pl.BlockSpec((ts, GLA_DV), lambda b, h, i: (tile(b, i), v_col0 + h)),
            pl.BlockSpec((ts, 2 * GATE_RANK), lambda b, h, i: (tile(b, i), 0)),
            pl.BlockSpec((2 * GATE_RANK, ts), lambda b, h, i: (0, tile(b, i))),
        ]

    fwd_tile = lambda b, i: b * n_s + i
    bwd_tile = lambda b, i: b * n_s + (n_s - 1 - i)
    gate_specs = [
        pl.BlockSpec((2 * GATE_RANK, GLA_DK), lambda b, h, i: (0, h)),
        pl.BlockSpec((GLA_DK, 2 * GATE_RANK), lambda b, h, i: (h, 0)),
        pl.BlockSpec((1, GLA_DK), lambda b, h, i: (0, h)),
        pl.BlockSpec((GLA_DK, 1), lambda b, h, i: (h, 0)),
    ]
    return pl.pallas_call(
        _gla_kernel,
        name="gla",
        grid=(batch, GLA_HEADS, n_s),
        in_specs=tile_specs(fwd_tile) + tile_specs(bwd_tile) + gate_specs + gate_specs,
        out_specs=[
            pl.BlockSpec((ts, GLA_DV), lambda b, h, i: (fwd_tile(b, i), h)),
            pl.BlockSpec((ts, GLA_DV), lambda b, h, i: (bwd_tile(b, i), h)),
        ],
        out_shape=[jax.ShapeDtypeStruct((t_rows, GLA_VALUE_WIDTH), F32)] * 2,
        scratch_shapes=[pltpu.VMEM((GLA_DK, GLA_DV), F32), pltpu.VMEM((GLA_DK, GLA_DV), F32)],
        compiler_params=_params(("parallel", "parallel", "arbitrary")),
    )(proj, kt, proj, z, zt, proj, kt, proj, z, zt, upf, uptf, bf, btf, upb, uptb, bb, btb)


def _outproj_kernel(o1_ref, o2_ref, o3_ref, l1_ref, l2_ref, l3_ref, of_ref, ob_ref, gr_ref, x_ref,
                    ag_ref, gg_ref, w_ref, h_ref, ao_ref, cat_a, cat_b):
    i = pl.program_id(0)
    rows = x_ref.shape[0]

    @pl.when(i == 0)
    def _():
        cat_b[...] = jnp.zeros_like(cat_b)

    def step(cur, prev):
        l1, l2, l3 = l1_ref[...], l2_ref[...], l3_ref[...]
        m = jnp.maximum(jnp.maximum(l1, l2), l3)
        e1, e2, e3 = jnp.exp(l1 - m), jnp.exp(l2 - m), jnp.exp(l3 - m)
        inv = 1.0 / (e1 + e2 + e3)
        w1, w2, w3 = e1 * inv, e2 * inv, e3 * inv

        ssq = jnp.zeros((rows, 1), F32)
        for h in range(ATTN_HEADS):
            hs = slice(h * HEAD_DIM, (h + 1) * HEAD_DIM)
            lc = (h // ATTN_HEAD_GROUP) * LANES + h % ATTN_HEAD_GROUP
            ao = (w1[:, lc:lc + 1] * o1_ref[:, hs].astype(F32) + w2[:, lc:lc + 1] * o2_ref[:, hs].astype(F32)
                  + w3[:, lc:lc + 1] * o3_ref[:, hs].astype(F32))
            ssq = ssq + jnp.sum(ao * ao, axis=-1, keepdims=True)
            ao_ref[:, hs] = ao
        inv_rms = lax.rsqrt(ssq * (1.0 / ATTN_WIDTH) + EPS)
        cur[:, 0:ATTN_WIDTH] = (ao_ref[...] * inv_rms * ag_ref[...]).astype(BF16)

        for h in range(GLA_HEADS):
            hs = slice(h * GLA_DV, (h + 1) * GLA_DV)
            o = of_ref[:, hs] + ob_ref[:, hs]
            y = o * lax.rsqrt(jnp.mean(o * o, axis=-1, keepdims=True) + EPS) * gg_ref[...]
            gate = gr_ref[:, hs].astype(F32)
            y = y * (gate / (1.0 + jnp.exp(-gate)))
            cur[:, ATTN_WIDTH + h * GLA_DV:ATTN_WIDTH + (h + 1) * GLA_DV] = y.astype(BF16)

        h_ref[...] = x_ref[...] + _bdot(prev[...], w_ref[...])

    @pl.when(i % 2 == 0)
    def _():
        step(cat_a, cat_b)

    @pl.when(i % 2 == 1)
    def _():
        step(cat_b, cat_a)


def _outproj(branches, o_f, o_b, proj, x2, attn_g, gla_g, w_out):
    t_rows = x2.shape[0]
    tm = OUTPROJ_ROWS
    n_tiles = t_rows // tm
    mixed = lambda i: jnp.minimum(i, n_tiles - 1)
    projected = lambda i: jnp.maximum(i - 1, 0)
    mix = lambda width: pl.BlockSpec((tm, width), lambda i: (mixed(i), 0))
    (o1, l1), (o2, l2), (o3, l3) = branches
    lse_w = ATTN_GROUPS * LANES
    cat = pltpu.VMEM((tm, D_MODEL), BF16)
    return pl.pallas_call(
        _outproj_kernel,
        name="outproj",
        grid=(n_tiles + 1,),
        in_specs=[
            mix(ATTN_WIDTH), mix(ATTN_WIDTH), mix(ATTN_WIDTH), mix(lse_w), mix(lse_w), mix(lse_w),
            mix(GLA_VALUE_WIDTH), mix(GLA_VALUE_WIDTH),
            pl.BlockSpec((tm, GLA_VALUE_WIDTH), lambda i: (mixed(i), GLA_GATE_COL // GLA_VALUE_WIDTH)),
            pl.BlockSpec((tm, D_MODEL), lambda i: (projected(i), 0)),
            pl.BlockSpec((1, ATTN_WIDTH), lambda i: (0, 0)),
            pl.BlockSpec((1, GLA_DV), lambda i: (0, 0)),
            pl.BlockSpec((D_MODEL, D_MODEL), lambda i: (0, 0)),
        ],
        out_specs=pl.BlockSpec((tm, D_MODEL), lambda i: (projected(i), 0)),
        out_shape=jax.ShapeDtypeStruct((t_rows, D_MODEL), F32),
        scratch_shapes=[pltpu.VMEM((tm, ATTN_WIDTH), F32), cat, cat],
        compiler_params=_params(("arbitrary",)),
    )(o1, o2, o3, l1, l2, l3, o_f, o_b, proj, x2, attn_g, gla_g, w_out)


def _ffn_kernel(h_ref, hp_ref, hn_ref, g2_ref, wg_ref, wu_ref, cw_ref, cb_ref, wd_ref, gf_ref,
                out_ref, n_ref, gate_ref, *, tiles_per_seq, final_norm):
    i = pl.program_id(0)
    j = pl.program_id(1)
    tm = h_ref.shape[0]
    halo = FFN_HALO

    def norm(x):
        return (x * lax.rsqrt(jnp.mean(x * x, axis=-1, keepdims=True) + EPS) * g2_ref[...]).astype(BF16)

    @pl.when(j == 0)
    def _():
        n_ref[0:halo] = norm(hp_ref[...])
        n_ref[halo:halo + tm] = norm(h_ref[...])
        n_ref[halo + tm:] = norm(hn_ref[...])
        out_ref[...] = h_ref[...]

    gate_ref[...] = _bdot(n_ref[...], wg_ref[...])
    row = lax.broadcasted_iota(jnp.int32, (tm, 1), 0)
    seq_tile = i % tiles_per_seq
    g_prev = jnp.where((row == 0) & (seq_tile == 0), 0.0, gate_ref[halo - 1:halo - 1 + tm, :])
    g_next = jnp.where((row == tm - 1) & (seq_tile == tiles_per_seq - 1), 0.0, gate_ref[halo + 1:halo + 1 + tm, :])
    g_mid = gate_ref[halo:halo + tm, :]
    cw = cw_ref[...]
    conv = g_prev * cw[0:1, :] + g_mid * cw[1:2, :] + g_next * cw[2:3, :] + cb_ref[...]
    up = _bdot(n_ref[halo:halo + tm, :], wu_ref[...])
    act = (conv / (1.0 + jnp.exp(-conv))) * up
    out_ref[...] += _bdot(act.astype(BF16), wd_ref[...])

    if final_norm:
        @pl.when(j == pl.num_programs(1) - 1)
        def _():
            y = out_ref[...]
            out_ref[...] = y * lax.rsqrt(jnp.mean(y * y, axis=-1, keepdims=True) + EPS) * gf_ref[...]


def _ffn(h2, g2, wg, wu, conv_w, conv_b, wd, gfinal, seq, final_norm):
    t_rows = h2.shape[0]
    tm = min(FFN_ROWS, seq)
    tf = FFN_COLS
    halo_blocks_per_tile = tm // FFN_HALO
    n_halo_blocks = t_rows // FFN_HALO
    return pl.pallas_call(
        functools.partial(_ffn_kernel, tiles_per_seq=seq // tm, final_norm=final_norm),
        name="ffn",
        grid=(t_rows // tm, D_FF // tf),
        in_specs=[
            pl.BlockSpec((tm, D_MODEL), lambda i, j: (i, 0), pipeline_mode=pl.Buffered(1)),
            pl.BlockSpec((FFN_HALO, D_MODEL), lambda i, j: (jnp.maximum(i * halo_blocks_per_tile - 1, 0), 0)),
            pl.BlockSpec((FFN_HALO, D_MODEL),
                         lambda i, j: (jnp.minimum((i + 1) * halo_blocks_per_tile, n_halo_blocks - 1), 0)),
            pl.BlockSpec((1, D_MODEL), lambda i, j: (0, 0)),
            pl.BlockSpec((D_MODEL, tf), lambda i, j: (0, j)),
            pl.BlockSpec((D_MODEL, tf), lambda i, j: (0, j)),
            pl.BlockSpec((CONV_WIDTH, tf), lambda i, j: (0, j)),
            pl.BlockSpec((1, tf), lambda i, j: (0, j)),
            pl.BlockSpec((tf, D_MODEL), lambda i, j: (j, 0)),
            pl.BlockSpec((1, D_MODEL), lambda i, j: (0, 0)),
        ],
        out_specs=pl.BlockSpec((tm, D_MODEL), lambda i, j: (i, 0)),
        out_shape=jax.ShapeDtypeStruct((t_rows, D_MODEL), F32),
        scratch_shapes=[pltpu.VMEM((tm + 2 * FFN_HALO, D_MODEL), BF16),
                        pltpu.VMEM((tm + 2 * FFN_HALO, tf), F32)],
        compiler_params=_params(("parallel", "arbitrary")),
    )(h2, h2, h2, g2, wg, wu, conv_w, conv_b, wd, gfinal)


def _rope_tables(seq, identity_rows):
    pos = np.arange(seq, dtype=np.float32)
    inv_freq = np.float32(ROPE_THETA) ** (-np.arange(0, ROPE_DIM, 2, dtype=np.float32) / np.float32(ROPE_DIM))
    ang = (pos[:, None] * inv_freq.astype(np.float32)[None, :]).astype(np.float32)
    cos = np.cos(ang.astype(np.float64)).astype(np.float32)
    sin = np.sin(ang.astype(np.float64)).astype(np.float32)
    gap = ROPE_PARTNER_SHIFT - ROPE_HALF
    ones, zeros = np.ones((seq, gap), np.float32), np.zeros((seq, gap), np.float32)
    cos_t = np.concatenate([cos, ones, cos, ones], axis=1)
    sin_t = np.concatenate([-sin, zeros, sin, zeros], axis=1)
    cos_t = np.concatenate([cos_t, np.ones((identity_rows, HEAD_DIM), np.float32)], axis=0)
    sin_t = np.concatenate([sin_t, np.zeros((identity_rows, HEAD_DIM), np.float32)], axis=0)
    return jnp.asarray(cos_t), jnp.asarray(sin_t)


def _rotary_head_layout(w):
    d_in, width = w.shape
    w = w.reshape(d_in, width // HEAD_DIM, HEAD_DIM)
    split = ROPE_DIM + ROPE_PARTNER_SHIFT - ROPE_HALF
    w = jnp.concatenate([w[..., :ROPE_HALF], w[..., ROPE_DIM:split], w[..., ROPE_HALF:ROPE_DIM], w[..., split:]],
                        axis=-1)
    return w.reshape(d_in, width)


def kernel(x, norm1_g, w_in, gf_up, gf_b, gb_up, gb_b, gla_norm_g, attn_norm_g, w_out, norm2_g, w_gate, w_up,
           conv_w, conv_b, w_down, final_norm_g):
    batch, seq, d_model = x.shape
    depth = norm1_g.shape[0]
    aw, kw, vw = ATTN_WIDTH, GLA_KEY_WIDTH, GLA_VALUE_WIDTH
    assert d_model == D_MODEL and w_in.shape[-1] == 3 * aw + 2 * kw + 2 * vw + 2 * GATE_RANK
    assert all((window // 2) // dil == N_SIDE for window, dil in DILATED_PATTERNS) and DILATIONS == (1, 4, 16)
    assert seq % ATTN_TILE == 0

    cos_t, sin_t = _rope_tables(seq, min(INPROJ_ROWS, seq))
    c_gq, c_gk, c_gv, c_gr, c_z = 3 * aw, 3 * aw + kw, 3 * aw + 2 * kw, 3 * aw + 2 * kw + vw, 3 * aw + 2 * kw + 2 * vw

    h2 = x.reshape(batch * seq, D_MODEL)
    for l in range(depth):
        wi = w_in[l]
        w_main = jnp.concatenate([_rotary_head_layout(wi[:, :2 * aw]), wi[:, 2 * aw:c_gq], wi[:, c_gv:c_gr],
                                  wi[:, c_gr:c_z], wi[:, c_gq:c_gk]], axis=1).astype(BF16)
        w_kz = jnp.concatenate([wi[:, c_gk:c_gv], wi[:, c_z:]], axis=1).T.astype(BF16)

        proj, proj4, proj16, kt, z, zt = _inproj(
            h2, norm1_g[l].reshape(1, D_MODEL), w_main, w_kz, cos_t, sin_t, batch, seq)
        by_residue = {1: proj.reshape(batch, 1, seq, PROJ_WIDTH), 4: proj4, 16: proj16}
        branches = [_attention_branch(by_residue[dil], batch, seq, dil) for dil in DILATIONS]
        o_f, o_b = _gla(proj, kt, z, zt, gf_up[l], gf_b[l], gb_up[l], gb_b[l], batch, seq)
        h2 = _outproj(branches, o_f, o_b, proj, h2, attn_norm_g[l].reshape(1, ATTN_WIDTH),
                      gla_norm_g[l].reshape(1, GLA_DV), w_out[l].astype(BF16))
        h2 = _ffn(h2, norm2_g[l].reshape(1, D_MODEL), w_gate[l].astype(BF16), w_up[l].astype(BF16), conv_w[l],
                  conv_b[l].reshape(1, D_FF), w_down[l].astype(BF16), final_norm_g.reshape(1, D_MODEL), seq,
                  final_norm=(l == depth - 1))
    return h2.reshape(batch, seq, D_MODEL)
```

```python
import functools

import jax
import jax.numpy as jnp
import numpy as np
from jax import lax
from jax.experimental import pallas as pl
from jax.experimental.pallas import tpu as pltpu

F32 = jnp.float32
BF16 = jnp.bfloat16

D_MODEL = 2048
ATTN_WIDTH = 1024
HEAD_DIM = 128
ATTN_HEADS = ATTN_WIDTH // HEAD_DIM
DILATED_PATTERNS = ((128, 1), (512, 4), (2048, 16))
DILATIONS = tuple(d for _, d in DILATED_PATTERNS)
N_SIDE = 64
ROPE_THETA = 500000.0
ROPE_DIM = HEAD_DIM // 4
ROPE_HALF = ROPE_DIM // 2
GLA_HEADS = 4
GLA_DK = 128
GLA_DV = 256
GLA_KEY_WIDTH = GLA_HEADS * GLA_DK
GLA_VALUE_WIDTH = GLA_HEADS * GLA_DV
GATE_RANK = 16
GATE_NORMALIZER = 16.0
GLA_CHUNK = 64
D_FF = 5632
CONV_WIDTH = 3
EPS = 1e-6

LANES = 128
BF16_SUBLANES = 16
VMEM_LIMIT_BYTES = 56 * 1024 * 1024

INPROJ_ROWS = 1024
INPROJ_COLS = 512
INPROJ_NORM_CHUNKS = 4
ATTN_TILE = 2048
ATTN_QBLK = 128
ATTN_HEAD_GROUP = 4
ATTN_GROUP_WIDTH = ATTN_HEAD_GROUP * HEAD_DIM
ATTN_GROUPS = ATTN_HEADS // ATTN_HEAD_GROUP
GLA_ROWS = 2048
GLA_PAIR = 2 * GLA_CHUNK
OUTPROJ_ROWS = 256
FFN_ROWS = 1024
FFN_COLS = 512
FFN_HALO = BF16_SUBLANES

ATTN_PROJ_WIDTH = 3 * ATTN_WIDTH
PROJ_WIDTH = ATTN_PROJ_WIDTH + 2 * GLA_VALUE_WIDTH + GLA_KEY_WIDTH
N_COL_TILES = PROJ_WIDTH // INPROJ_COLS
N_ATTN_COL_TILES = ATTN_PROJ_WIDTH // INPROJ_COLS
RESIDUE_WIDTH = ATTN_PROJ_WIDTH
GLA_V_COL = ATTN_PROJ_WIDTH
GLA_GATE_COL = GLA_V_COL + GLA_VALUE_WIDTH
GLA_Q_COL = GLA_GATE_COL + GLA_VALUE_WIDTH
ROPE_PARTNER_SHIFT = HEAD_DIM // 2

_NT = (((1,), (1,)), ((), ()))


def _params(semantics):
    return pltpu.CompilerParams(dimension_semantics=semantics, vmem_limit_bytes=VMEM_LIMIT_BYTES)


def _bdot(a, b):
    return jnp.dot(a, b, preferred_element_type=F32)


def _inproj_kernel(x_ref, g_ref, w_ref, wkz_ref, cos_ref, sin_ref,
                   p_ref, p4_ref, p16_ref, kt_ref, z_ref, n_ref, rs_a, rs_b, g4_ref):
    i = pl.program_id(0)
    j = pl.program_id(1)
    rows = x_ref.shape[0]
    heads_per_tile = INPROJ_COLS // HEAD_DIM

    @pl.when((i == 0) & (j == 0))
    def _():
        rs_b[...] = jnp.zeros_like(rs_b)

    @pl.when(j == 0)
    def _():
        chunk = rows // INPROJ_NORM_CHUNKS
        for c in range(INPROJ_NORM_CHUNKS):
            rs = slice(c * chunk, (c + 1) * chunk)
            x = x_ref[rs, :]
            inv = lax.rsqrt(jnp.mean(x * x, axis=-1, keepdims=True) + EPS)
            n = (x * inv * g_ref[...]).astype(BF16)
            n_ref[rs, :] = n
            kz = lax.dot_general(wkz_ref[...], n, _NT, preferred_element_type=F32)
            kt_ref[:, rs] = kz[0:GLA_KEY_WIDTH, :].astype(BF16)
            z_ref[rs, :] = kz[GLA_KEY_WIDTH:, :].T

    done = jnp.maximum(j - 1, 0)
    scaled = (done < ATTN_WIDTH // INPROJ_COLS) | (done == N_COL_TILES - 1)
    scale = jnp.where(scaled, HEAD_DIM ** -0.5, 1.0).astype(F32)

    def step(cur, prev, regroup):
        cos = cos_ref[...]
        sin = sin_ref[...]
        for h in range(heads_per_tile):
            sl = slice(h * HEAD_DIM, (h + 1) * HEAD_DIM)
            t = prev[h]
            r = (t * cos + pltpu.roll(t, ROPE_PARTNER_SHIFT, 1) * sin) * scale
            p_ref[:, sl] = r.astype(BF16)
            if regroup:
                prev[h] = r
        if regroup:
            per4 = rows // 4
            per16 = rows // 16
            for h in range(heads_per_tile):
                sl = slice(h * HEAD_DIM, (h + 1) * HEAD_DIM)
                for res4 in range(4):
                    grp = prev[h, pl.ds(res4, per4, stride=4), :]
                    p4_ref[0, res4, :, sl] = grp.astype(BF16)
                    g4_ref[h, res4 * per4:(res4 + 1) * per4, :] = grp
                for res4 in range(4):
                    for a in range(4):
                        p16_ref[0, 4 * a + res4, :, sl] = (
                            g4_ref[h, pl.ds(res4 * per4 + a, per16, stride=4), :].astype(BF16))
        acc = _bdot(n_ref[...], w_ref[...])
        for h in range(heads_per_tile):
            cur[h] = acc[:, h * HEAD_DIM:(h + 1) * HEAD_DIM]

    attn_tile = (j >= 1) & (j <= N_ATTN_COL_TILES)
    for parity, (cur, prev) in enumerate(((rs_a, rs_b), (rs_b, rs_a))):
        for regroup in (True, False):
            @pl.when((j % 2 == parity) & (attn_tile if regroup else jnp.logical_not(attn_tile)))
            def _(cur=cur, prev=prev, regroup=regroup):
                step(cur, prev, regroup)


def _inproj(x2, g1, w_main, w_kz, cos_t, sin_t, batch, seq):
    t_rows = x2.shape[0]
    tm = min(INPROJ_ROWS, seq)
    tps = seq // tm
    done = lambda j: jnp.maximum(j - 1, 0)
    n_rotary_tiles = 2 * ATTN_WIDTH // INPROJ_COLS
    table_spec = pl.BlockSpec((tm, HEAD_DIM), lambda i, j: (jnp.where(done(j) < n_rotary_tiles, i % tps, tps), 0))

    def residue_spec(dil):
        return pl.BlockSpec((1, dil, tm // dil, INPROJ_COLS),
                            lambda i, j: (i // tps, 0, i % tps, jnp.minimum(done(j), N_ATTN_COL_TILES - 1)))

    slab = pltpu.VMEM((INPROJ_COLS // HEAD_DIM, tm, HEAD_DIM), F32)
    return pl.pallas_call(
        _inproj_kernel,
        name="inproj",
        grid=(t_rows // tm, N_COL_TILES + 1),
        in_specs=[
            pl.BlockSpec((tm, D_MODEL), lambda i, j: (i, 0)),
            pl.BlockSpec((1, D_MODEL), lambda i, j: (0, 0)),
            pl.BlockSpec((D_MODEL, INPROJ_COLS), lambda i, j: (0, jnp.minimum(j, N_COL_TILES - 1))),
            pl.BlockSpec((GLA_KEY_WIDTH + 2 * GATE_RANK, D_MODEL), lambda i, j: (0, 0)),
            table_spec,
            table_spec,
        ],
        out_specs=[
            pl.BlockSpec((tm, INPROJ_COLS), lambda i, j: (i, done(j))),
            residue_spec(4),
            residue_spec(16),
            pl.BlockSpec((GLA_KEY_WIDTH, tm), lambda i, j: (0, i)),
            pl.BlockSpec((tm, 2 * GATE_RANK), lambda i, j: (i, 0)),
        ],
        out_shape=[
            jax.ShapeDtypeStruct((t_rows, PROJ_WIDTH), BF16),
            jax.ShapeDtypeStruct((batch, 4, seq // 4, RESIDUE_WIDTH), BF16),
            jax.ShapeDtypeStruct((batch, 16, seq // 16, RESIDUE_WIDTH), BF16),
            jax.ShapeDtypeStruct((GLA_KEY_WIDTH, t_rows), BF16),
            jax.ShapeDtypeStruct((t_rows, 2 * GATE_RANK), F32),
        ],
        scratch_shapes=[pltpu.VMEM((tm, D_MODEL), BF16), slab, slab, slab],
        compiler_params=_params(("arbitrary", "arbitrary")),
    )(x2, g1, w_main, w_kz, cos_t, sin_t)


def _attn_kernel(q_ref, kp_ref, kc_ref, kn_ref, vp_ref, vc_ref, vn_ref, o_ref, lse_ref, osc, *, length, dil):
    t = pl.program_id(2)
    tq = q_ref.shape[2]
    win = ATTN_QBLK + 2 * N_SIDE
    n_blk = tq // ATTN_QBLK

    def window(prev_ref, cur_ref, next_ref, res, blk, cols):
        lo, hi = blk * ATTN_QBLK - N_SIDE, blk * ATTN_QBLK + ATTN_QBLK + N_SIDE
        parts = []
        if lo < 0:
            parts.append(prev_ref[0, res, :, cols])
        parts.append(cur_ref[0, res, max(lo, 0):min(hi, tq), cols])
        if hi > tq:
            parts.append(next_ref[0, res, :, cols])
        return parts[0] if len(parts) == 1 else jnp.concatenate(parts, axis=0)

    ones = jnp.ones((win, HEAD_DIM), BF16)

    row = lax.broadcasted_iota(jnp.int32, (ATTN_QBLK, win), 0)
    col = lax.broadcasted_iota(jnp.int32, (ATTN_QBLK, win), 1)
    band = (col >= row) & (col <= row + 2 * N_SIDE)
    lane = lax.broadcasted_iota(jnp.int32, (ATTN_QBLK, LANES), 1)

    for res, blk in [(a, b) for a in range(dil) for b in range(n_blk)]:
        r0 = blk * ATTN_QBLK
        first_key = t * tq + r0 - N_SIDE
        if 0 < blk < n_blk - 1:
            valid = band
        else:
            valid = band & (col >= -first_key) & (col < length - first_key)
        if dil == 1:
            dst = pl.ds(r0, ATTN_QBLK)
        else:
            dst = pl.ds(r0 * dil + res, ATTN_QBLK, stride=dil)
        lse_tile = jnp.zeros((ATTN_QBLK, LANES), F32)
        for h in range(ATTN_HEAD_GROUP):
            hs = slice(h * HEAD_DIM, (h + 1) * HEAD_DIM)
            q = q_ref[0, res, pl.ds(r0, ATTN_QBLK), hs]
            k = window(kp_ref, kc_ref, kn_ref, res, blk, hs)
            v_ones = jnp.concatenate([window(vp_ref, vc_ref, vn_ref, res, blk, hs), ones], axis=1)
            s = lax.dot_general(q, k, _NT, preferred_element_type=F32)
            s = jnp.where(valid, s, -1e30)
            m = jnp.max(s, axis=-1, keepdims=True)
            p = jnp.exp(s - m)
            o_den = _bdot(p.astype(BF16), v_ones)
            den = o_den[:, HEAD_DIM:]
            osc[h, dst, :] = o_den[:, :HEAD_DIM] / den
            lse_tile = jnp.where(lane == h, m + jnp.log(den), lse_tile)
        lse_ref[dst, :] = lse_tile

    for h in range(ATTN_HEAD_GROUP):
        o_ref[:, h * HEAD_DIM:(h + 1) * HEAD_DIM] = osc[h].astype(o_ref.dtype)


def _attention_branch(proj_res, batch, seq, dil):
    length = seq // dil
    tn = min(ATTN_TILE, seq)
    tq = tn // dil
    n_tiles = seq // tn
    halo_per_tile = tq // N_SIDE
    n_halo_blocks = length // N_SIDE
    gw = ATTN_GROUP_WIDTH
    part_blocks = ATTN_WIDTH // gw

    def cur(part):
        return pl.BlockSpec((1, dil, tq, gw), lambda b, g, t: (b, 0, t, part * part_blocks + g))

    def prev(part):
        return pl.BlockSpec((1, dil, N_SIDE, gw),
                            lambda b, g, t: (b, 0, jnp.maximum(t * halo_per_tile - 1, 0), part * part_blocks + g))

    def nxt(part):
        return pl.BlockSpec((1, dil, N_SIDE, gw),
                            lambda b, g, t: (b, 0, jnp.minimum((t + 1) * halo_per_tile, n_halo_blocks - 1),
                                             part * part_blocks + g))

    return pl.pallas_call(
        functools.partial(_attn_kernel, length=length, dil=dil),
        name=f"attn_d{dil}",
        grid=(batch, ATTN_GROUPS, n_tiles),
        in_specs=[cur(0), prev(1), cur(1), nxt(1), prev(2), cur(2), nxt(2)],
        out_specs=[
            pl.BlockSpec((tn, gw), lambda b, g, t: (b * n_tiles + t, g)),
            pl.BlockSpec((tn, LANES), lambda b, g, t: (b * n_tiles + t, g)),
        ],
        out_shape=[
            jax.ShapeDtypeStruct((batch * seq, ATTN_WIDTH), BF16),
            jax.ShapeDtypeStruct((batch * seq, ATTN_GROUPS * LANES), F32),
        ],
        scratch_shapes=[pltpu.VMEM((ATTN_HEAD_GROUP, tn, HEAD_DIM), F32)],
        compiler_params=_params(("parallel", "parallel", "parallel")),
    )(proj_res, proj_res, proj_res, proj_res, proj_res, proj_res, proj_res)


def _split2(a):
    hi = a.astype(BF16)
    lo = (a - hi.astype(F32)).astype(BF16)
    return hi, lo


def _dot_f32(a, b):
    a_hi, a_lo = _split2(a)
    b_hi, b_lo = _split2(b)
    return _bdot(a_hi, b_hi) + (_bdot(a_hi, b_lo) + _bdot(a_lo, b_hi))


def _log_sigmoid(x):
    return jnp.minimum(x, 0.0) - jnp.log(1.0 + jnp.exp(-jnp.abs(x)))


class _Direction:
    def __init__(self, refs, forward):
        self.q_ref, self.kt_ref, self.v_ref, z_ref, up_ref, bias_ref, self.o_ref, self.state_ref = refs
        c = GLA_CHUNK
        r = lax.broadcasted_iota(jnp.int32, (GLA_PAIR, GLA_PAIR), 0)
        cc = lax.broadcasted_iota(jnp.int32, (GLA_PAIR, GLA_PAIR), 1)
        same = jnp.where(r < c, 0, 1) == jnp.where(cc < c, 0, 1)
        lower = same & (r >= cc)
        upper = same & (r <= cc)
        self.row_low = r < c
        self.col_low = cc < c
        if forward:
            left, right, self.causal = lower, upper, lower
            self.ref_i, self.last_i = c // 2 - 1, c - 1
            self.cross = (r >= c) & (cc < c)
            self.i_first, self.i_second = self.last_i, c + self.last_i
            self.second_rows, self.first_cols = r >= c, cc < c
        else:
            left, right, self.causal = upper, lower, upper
            self.ref_i, self.last_i = c // 2, 0
            self.cross = (r < c) & (cc >= c)
            self.i_first, self.i_second = c + self.last_i, self.last_i
            self.second_rows, self.first_cols = r < c, cc >= c
        self.left_m = jnp.where(left, 1.0, 0.0).astype(BF16)
        self.right_m = jnp.where(right, 1.0, 0.0).astype(BF16)
        inv_norm = 1.0 / GATE_NORMALIZER
        self.g = _log_sigmoid(_dot_f32(z_ref[...], up_ref[...]) + bias_ref[...]) * inv_norm


def _gla_kernel(qf_ref, ktf_ref, vf_ref, zf_ref, qb_ref, ktb_ref, vb_ref, zb_ref,
                upf_ref, bf_ref, upb_ref, bb_ref, of_ref, ob_ref, sf_ref, sb_ref):
    @pl.when(pl.program_id(2) == 0)
    def _():
        sf_ref[...] = jnp.zeros_like(sf_ref)
        sb_ref[...] = jnp.zeros_like(sb_ref)

    c = GLA_CHUNK
    ts = qf_ref.shape[0]
    n_pairs = ts // GLA_PAIR
    fwd = _Direction((qf_ref, ktf_ref, vf_ref, zf_ref, upf_ref, bf_ref, of_ref, sf_ref), True)
    bwd = _Direction((qb_ref, ktb_ref, vb_ref, zb_ref, upb_ref, bb_ref, ob_ref, sb_ref), False)
    jobs = []
    for s in range(n_pairs):
        jobs.append((fwd, slice(s * GLA_PAIR, (s + 1) * GLA_PAIR)))
        jobs.append((bwd, slice((n_pairs - 1 - s) * GLA_PAIR, (n_pairs - s) * GLA_PAIR)))

    cums = []
    for d, rows in jobs:
        g = d.g[rows]
        g_hi, g_lo = _split2(g)
        gt_hi, gt_lo = _split2(g.T)
        b = _bdot(d.left_m, g_hi) + _bdot(d.left_m, g_lo)
        bt = _bdot(gt_hi, d.right_m) + _bdot(gt_lo, d.right_m)
        cums.append((b, bt))

    ops = []
    for (d, rows), (b, bt) in zip(jobs, cums):
        def rowpick(a, i):
            return jnp.where(d.row_low, a[i:i + 1, :], a[c + i:c + i + 1, :])

        def colpick(a, i):
            return jnp.where(d.col_low, a[:, i:i + 1], a[:, c + i:c + i + 1])

        q = d.q_ref[rows, :].astype(F32)
        kt = d.kt_ref[:, rows].astype(F32)
        qe = (q * jnp.exp(b - rowpick(b, d.ref_i))).astype(BF16)
        q_in = q * jnp.exp(b)
        ket = (kt * jnp.exp(colpick(bt, d.ref_i) - bt)).astype(BF16)
        kst = kt * jnp.exp(colpick(bt, d.last_i) - bt)
        dec_first_row = jnp.exp(b[d.i_first:d.i_first + 1, :])
        dec_second_col = jnp.exp(bt[:, d.i_second:d.i_second + 1])
        dec_total_col = jnp.exp(bt[:, d.i_first:d.i_first + 1] + bt[:, d.i_second:d.i_second + 1])
        q_state = (q_in * jnp.where(d.second_rows, dec_first_row, 1.0)).astype(BF16)
        k_state = (kst * jnp.where(d.first_cols, dec_second_col, 1.0)).astype(BF16)
        ops.append((qe, ket, q_in.astype(BF16), kst.astype(BF16), q_state, k_state, dec_total_col))

    scores = []
    for qe, ket, q_in, kst, _, _, _ in ops:
        scores.append((_bdot(qe, ket), _bdot(q_in, kst)))

    locals_ = []
    for (d, rows), (a_diag, a_cross), op in zip(jobs, scores, ops):
        a = jnp.where(d.causal, a_diag, jnp.where(d.cross, a_cross, 0.0)).astype(BF16)
        v = d.v_ref[rows, :]
        locals_.append((_bdot(a, v), _bdot(op[5], v)))

    state = {id(fwd): sf_ref[...], id(bwd): sb_ref[...]}
    for (d, rows), (o_intra, upd), op in zip(jobs, locals_, ops):
        s = state[id(d)]
        d.o_ref[rows, :] = o_intra + _bdot(op[4], s.astype(BF16))
        state[id(d)] = op[6] * s + upd
    sf_ref[...] = state[id(fwd)]
    sb_ref[...] = state[id(bwd)]


def _gla(proj, kt, z, gf_up, gf_b, gb_up, gb_b, batch, seq):
    t_rows = batch * seq
    ts = min(GLA_ROWS, seq)
    n_s = seq // ts
    q_col0 = GLA_Q_COL // GLA_DK
    v_col0 = GLA_V_COL // GLA_DV

    zero = jnp.zeros((GATE_RANK, GLA_KEY_WIDTH), F32)
    upf = jnp.concatenate([gf_up, zero], axis=0)
    upb = jnp.concatenate([zero, gb_up], axis=0)
    bf = gf_b.reshape(1, GLA_KEY_WIDTH)
    bb = gb_b.reshape(1, GLA_KEY_WIDTH)

    def tile_specs(tile):
        return [
            pl.BlockSpec((ts, GLA_DK), lambda b, h, i: (tile(b, i), q_col0 + h)),
            pl.BlockSpec((GLA_DK, ts), lambda b, h, i: (h, tile(b, i))),
            pl.BlockSpec((ts, GLA_DV), lambda b, h, i: (tile(b, i), v_col0 + h)),
            pl.BlockSpec((ts, 2 * GATE_RANK), lambda b, h, i: (tile(b, i), 0)),
        ]

    fwd_tile = lambda b, i: b * n_s + i
    bwd_tile = lambda b, i: b * n_s + (n_s - 1 - i)
    gate_specs = [
        pl.BlockSpec((2 * GATE_RANK, GLA_DK), lambda b, h, i: (0, h)),
        pl.BlockSpec((1, GLA_DK), lambda b, h, i: (0, h)),
    ]
    return pl.pallas_call(
        _gla_kernel,
        name="gla",
        grid=(batch, GLA_HEADS, n_s),
        in_specs=tile_specs(fwd_tile) + tile_specs(bwd_tile) + gate_specs + gate_specs,
        out_specs=[
            pl.BlockSpec((ts, GLA_DV), lambda b, h, i: (fwd_tile(b, i), h)),
            pl.BlockSpec((ts, GLA_DV), lambda b, h, i: (bwd_tile(b, i), h)),
        ],
        out_shape=[jax.ShapeDtypeStruct((t_rows, GLA_VALUE_WIDTH), F32)] * 2,
        scratch_shapes=[pltpu.VMEM((GLA_DK, GLA_DV), F32), pltpu.VMEM((GLA_DK, GLA_DV), F32)],
        compiler_params=_params(("parallel", "parallel", "arbitrary")),
    )(proj, kt, proj, z, proj, kt, proj, z, upf, bf, upb, bb)


def _outproj_kernel(o1_ref, o2_ref, o3_ref, l1_ref, l2_ref, l3_ref, of_ref, ob_ref, gr_ref, x_ref,
                    ag_ref, gg_ref, w_ref, h_ref, ao_ref, cat_a, cat_b):
    i = pl.program_id(0)
    rows = x_ref.shape[0]

    @pl.when(i == 0)
    def _():
        cat_b[...] = jnp.zeros_like(cat_b)

    def step(cur, prev):
        l1, l2, l3 = l1_ref[...], l2_ref[...], l3_ref[...]
        m = jnp.maximum(jnp.maximum(l1, l2), l3)
        e1, e2, e3 = jnp.exp(l1 - m), jnp.exp(l2 - m), jnp.exp(l3 - m)
        inv = 1.0 / (e1 + e2 + e3)
        w1, w2, w3 = e1 * inv, e2 * inv, e3 * inv

        ssq = jnp.zeros((rows, 1), F32)
        for h in range(ATTN_HEADS):
            hs = slice(h * HEAD_DIM, (h + 1) * HEAD_DIM)
            lc = (h // ATTN_HEAD_GROUP) * LANES + h % ATTN_HEAD_GROUP
            ao = (w1[:, lc:lc + 1] * o1_ref[:, hs].astype(F32) + w2[:, lc:lc + 1] * o2_ref[:, hs].astype(F32)
                  + w3[:, lc:lc + 1] * o3_ref[:, hs].astype(F32))
            ssq = ssq + jnp.sum(ao * ao, axis=-1, keepdims=True)
            ao_ref[:, hs] = ao
        inv_rms = lax.rsqrt(ssq * (1.0 / ATTN_WIDTH) + EPS)
        cur[:, 0:ATTN_WIDTH] = (ao_ref[...] * inv_rms * ag_ref[...]).astype(BF16)

        for h in range(GLA_HEADS):
            hs = slice(h * GLA_DV, (h + 1) * GLA_DV)
            o = of_ref[:, hs] + ob_ref[:, hs]
            y = o * lax.rsqrt(jnp.mean(o * o, axis=-1, keepdims=True) + EPS) * gg_ref[...]
            gate = gr_ref[:, hs].astype(F32)
            y = y * (gate / (1.0 + jnp.exp(-gate)))
            cur[:, ATTN_WIDTH + h * GLA_DV:ATTN_WIDTH + (h + 1) * GLA_DV] = y.astype(BF16)

        h_ref[...] = x_ref[...] + _bdot(prev[...], w_ref[...])

    @pl.when(i % 2 == 0)
    def _():
        step(cat_a, cat_b)

    @pl.when(i % 2 == 1)
    def _():
        step(cat_b, cat_a)


def _outproj(branches, o_f, o_b, proj, x2, attn_g, gla_g, w_out):
    t_rows = x2.shape[0]
    tm = OUTPROJ_ROWS
    n_tiles = t_rows // tm
    mixed = lambda i: jnp.minimum(i, n_tiles - 1)
    projected = lambda i: jnp.maximum(i - 1, 0)
    mix = lambda width: pl.BlockSpec((tm, width), lambda i: (mixed(i), 0))
    (o1, l1), (o2, l2), (o3, l3) = branches
    lse_w = ATTN_GROUPS * LANES
    cat = pltpu.VMEM((tm, D_MODEL), BF16)
    return pl.pallas_call(
        _outproj_kernel,
        name="outproj",
        grid=(n_tiles + 1,),
        in_specs=[
            mix(ATTN_WIDTH), mix(ATTN_WIDTH), mix(ATTN_WIDTH), mix(lse_w), mix(lse_w), mix(lse_w),
            mix(GLA_VALUE_WIDTH), mix(GLA_VALUE_WIDTH),
            pl.BlockSpec((tm, GLA_VALUE_WIDTH), lambda i: (mixed(i), GLA_GATE_COL // GLA_VALUE_WIDTH)),
            pl.BlockSpec((tm, D_MODEL), lambda i: (projected(i), 0)),
            pl.BlockSpec((1, ATTN_WIDTH), lambda i: (0, 0)),
            pl.BlockSpec((1, GLA_DV), lambda i: (0, 0)),
            pl.BlockSpec((D_MODEL, D_MODEL), lambda i: (0, 0)),
        ],
        out_specs=pl.BlockSpec((tm, D_MODEL), lambda i: (projected(i), 0)),
        out_shape=jax.ShapeDtypeStruct((t_rows, D_MODEL), F32),
        scratch_shapes=[pltpu.VMEM((tm, ATTN_WIDTH), F32), cat, cat],
        compiler_params=_params(("arbitrary",)),
    )(o1, o2, o3, l1, l2, l3, o_f, o_b, proj, x2, attn_g, gla_g, w_out)


def _ffn_kernel(h_ref, hp_ref, hn_ref, g2_ref, wg_ref, wu_ref, cw_ref, cb_ref, wd_ref, gf_ref,
                out_ref, n_ref, gate_ref, *, tiles_per_seq, final_norm):
    i = pl.program_id(0)
    j = pl.program_id(1)
    tm = h_ref.shape[0]
    halo = FFN_HALO

    def norm(x):
        return (x * lax.rsqrt(jnp.mean(x * x, axis=-1, keepdims=True) + EPS) * g2_ref[...]).astype(BF16)

    @pl.when(j == 0)
    def _():
        n_ref[0:halo] = norm(hp_ref[...])
        n_ref[halo:halo + tm] = norm(h_ref[...])
        n_ref[halo + tm:] = norm(hn_ref[...])
        out_ref[...] = h_ref[...]

    gate_ref[...] = _bdot(n_ref[...], wg_ref[...])
    row = lax.broadcasted_iota(jnp.int32, (tm, 1), 0)
    seq_tile = i % tiles_per_seq
    g_prev = jnp.where((row == 0) & (seq_tile == 0), 0.0, gate_ref[halo - 1:halo - 1 + tm, :])
    g_next = jnp.where((row == tm - 1) & (seq_tile == tiles_per_seq - 1), 0.0, gate_ref[halo + 1:halo + 1 + tm, :])
    g_mid = gate_ref[halo:halo + tm, :]
    cw = cw_ref[...]
    conv = g_prev * cw[0:1, :] + g_mid * cw[1:2, :] + g_next * cw[2:3, :] + cb_ref[...]
    up = _bdot(n_ref[halo:halo + tm, :], wu_ref[...])
    act = (conv / (1.0 + jnp.exp(-conv))) * up
    out_ref[...] += _bdot(act.astype(BF16), wd_ref[...])

    if final_norm:
        @pl.when(j == pl.num_programs(1) - 1)
        def _():
            y = out_ref[...]
            out_ref[...] = y * lax.rsqrt(jnp.mean(y * y, axis=-1, keepdims=True) + EPS) * gf_ref[...]


def _ffn(h2, g2, wg, wu, conv_w, conv_b, wd, gfinal, seq, final_norm):
    t_rows = h2.shape[0]
    tm = min(FFN_ROWS, seq)
    tf = FFN_COLS
    halo_blocks_per_tile = tm // FFN_HALO
    n_halo_blocks = t_rows // FFN_HALO
    return pl.pallas_call(
        functools.partial(_ffn_kernel, tiles_per_seq=seq // tm, final_norm=final_norm),
        name="ffn",
        grid=(t_rows // tm, D_FF // tf),
        in_specs=[
            pl.BlockSpec((tm, D_MODEL), lambda i, j: (i, 0), pipeline_mode=pl.Buffered(1)),
            pl.BlockSpec((FFN_HALO, D_MODEL), lambda i, j: (jnp.maximum(i * halo_blocks_per_tile - 1, 0), 0)),
            pl.BlockSpec((FFN_HALO, D_MODEL),
                         lambda i, j: (jnp.minimum((i + 1) * halo_blocks_per_tile, n_halo_blocks - 1), 0)),
            pl.BlockSpec((1, D_MODEL), lambda i, j: (0, 0)),
            pl.BlockSpec((D_MODEL, tf), lambda i, j: (0, j)),
            pl.BlockSpec((D_MODEL, tf), lambda i, j: (0, j)),
            pl.BlockSpec((CONV_WIDTH, tf), lambda i, j: (0, j)),
            pl.BlockSpec((1, tf), lambda i, j: (0, j)),
            pl.BlockSpec((tf, D_MODEL), lambda i, j: (j, 0)),
            pl.BlockSpec((1, D_MODEL), lambda i, j: (0, 0)),
        ],
        out_specs=pl.BlockSpec((tm, D_MODEL), lambda i, j: (i, 0)),
        out_shape=jax.ShapeDtypeStruct((t_rows, D_MODEL), F32),
        scratch_shapes=[pltpu.VMEM((tm + 2 * FFN_HALO, D_MODEL), BF16),
                        pltpu.VMEM((tm + 2 * FFN_HALO, tf), F32)],
        compiler_params=_params(("parallel", "arbitrary")),
    )(h2, h2, h2, g2, wg, wu, conv_w, conv_b, wd, gfinal)


def _rope_tables(seq, identity_rows):
    pos = np.arange(seq, dtype=np.float32)
    inv_freq = np.float32(ROPE_THETA) ** (-np.arange(0, ROPE_DIM, 2, dtype=np.float32) / np.float32(ROPE_DIM))
    ang = (pos[:, None] * inv_freq.astype(np.float32)[None, :]).astype(np.float32)
    cos = np.cos(ang.astype(np.float64)).astype(np.float32)
    sin = np.sin(ang.astype(np.float64)).astype(np.float32)
    gap = ROPE_PARTNER_SHIFT - ROPE_HALF
    ones, zeros = np.ones((seq, gap), np.float32), np.zeros((seq, gap), np.float32)
    cos_t = np.concatenate([cos, ones, cos, ones], axis=1)
    sin_t = np.concatenate([-sin, zeros, sin, zeros], axis=1)
    cos_t = np.concatenate([cos_t, np.ones((identity_rows, HEAD_DIM), np.float32)], axis=0)
    sin_t = np.concatenate([sin_t, np.zeros((identity_rows, HEAD_DIM), np.float32)], axis=0)
    return jnp.asarray(cos_t), jnp.asarray(sin_t)


def _rotary_head_layout(w):
    d_in, width = w.shape
    w = w.reshape(d_in, width // HEAD_DIM, HEAD_DIM)
    split = ROPE_DIM + ROPE_PARTNER_SHIFT - ROPE_HALF
    w = jnp.concatenate([w[..., :ROPE_HALF], w[..., ROPE_DIM:split], w[..., ROPE_HALF:ROPE_DIM], w[..., split:]],
                        axis=-1)
    return w.reshape(d_in, width)


def kernel(x, norm1_g, w_in, gf_up, gf_b, gb_up, gb_b, gla_norm_g, attn_norm_g, w_out, norm2_g, w_gate, w_up,
           conv_w, conv_b, w_down, final_norm_g):
    batch, seq, d_model = x.shape
    depth = norm1_g.shape[0]
    aw, kw, vw = ATTN_WIDTH, GLA_KEY_WIDTH, GLA_VALUE_WIDTH
    assert d_model == D_MODEL and w_in.shape[-1] == 3 * aw + 2 * kw + 2 * vw + 2 * GATE_RANK
    assert all((window // 2) // dil == N_SIDE for window, dil in DILATED_PATTERNS) and DILATIONS == (1, 4, 16)
    assert seq % ATTN_TILE == 0

    cos_t, sin_t = _rope_tables(seq, min(INPROJ_ROWS, seq))
    c_gq, c_gk, c_gv, c_gr, c_z = 3 * aw, 3 * aw + kw, 3 * aw + 2 * kw, 3 * aw + 2 * kw + vw, 3 * aw + 2 * kw + 2 * vw

    h2 = x.reshape(batch * seq, D_MODEL)
    for l in range(depth):
        wi = w_in[l]
        w_main = jnp.concatenate([_rotary_head_layout(wi[:, :2 * aw]), wi[:, 2 * aw:c_gq], wi[:, c_gv:c_gr],
                                  wi[:, c_gr:c_z], wi[:, c_gq:c_gk]], axis=1).astype(BF16)
        w_kz = jnp.concatenate([wi[:, c_gk:c_gv], wi[:, c_z:]], axis=1).T.astype(BF16)

        proj, proj4, proj16, kt, z = _inproj(
            h2, norm1_g[l].reshape(1, D_MODEL), w_main, w_kz, cos_t, sin_t, batch, seq)
        by_residue = {1: proj.reshape(batch, 1, seq, PROJ_WIDTH), 4: proj4, 16: proj16}
        branches = [_attention_branch(by_residue[dil], batch, seq, dil) for dil in DILATIONS]
        o_f, o_b = _gla(proj, kt, z, gf_up[l], gf_b[l], gb_up[l], gb_b[l], batch, seq)
        h2 = _outproj(branches, o_f, o_b, proj, h2, attn_norm_g[l].reshape(1, ATTN_WIDTH),
                      gla_norm_g[l].reshape(1, GLA_DV), w_out[l].astype(BF16))
        h2 = _ffn(h2, norm2_g[l].reshape(1, D_MODEL), w_gate[l].astype(BF16), w_up[l].astype(BF16), conv_w[l],
                  conv_b[l].reshape(1, D_FF), w_down[l].astype(BF16), final_norm_g.reshape(1, D_MODEL), seq,
                  final_norm=(l == depth - 1))
    return h2.reshape(batch, seq, D_MODEL)
```

```python
import functools

import jax
import jax.numpy as jnp
import numpy as np
from jax import lax
from jax.experimental import pallas as pl
from jax.experimental.pallas import tpu as pltpu

F32 = jnp.float32
BF16 = jnp.bfloat16

D_MODEL = 2048
ATTN_WIDTH = 1024
HEAD_DIM = 128
ATTN_HEADS = ATTN_WIDTH // HEAD_DIM
DILATED_PATTERNS = ((128, 1), (512, 4), (2048, 16))
DILATIONS = tuple(d for _, d in DILATED_PATTERNS)
N_SIDE = 64
ROPE_THETA = 500000.0
ROPE_DIM = HEAD_DIM // 4
ROPE_HALF = ROPE_DIM // 2
GLA_HEADS = 4
GLA_DK = 128
GLA_DV = 256
GLA_KEY_WIDTH = GLA_HEADS * GLA_DK
GLA_VALUE_WIDTH = GLA_HEADS * GLA_DV
GATE_RANK = 16
GATE_NORMALIZER = 16.0
GLA_CHUNK = 64
D_FF = 5632
CONV_WIDTH = 3
EPS = 1e-6

LANES = 128
BF16_SUBLANES = 16
VMEM_LIMIT_BYTES = 60 * 1024 * 1024

INPROJ_ROWS = 1024
INPROJ_COLS = 512
INPROJ_NORM_CHUNKS = 4
ATTN_TILE = 2048
ATTN_QBLK = 128
ATTN_HEAD_GROUP = 4
ATTN_GROUP_WIDTH = ATTN_HEAD_GROUP * HEAD_DIM
ATTN_GROUPS = ATTN_HEADS // ATTN_HEAD_GROUP
GLA_ROWS = 2048
GLA_PAIR = 2 * GLA_CHUNK
OUTPROJ_ROWS = 512
FFN_ROWS = 1024
FFN_COLS = 512
FFN_HALO = BF16_SUBLANES

ATTN_PROJ_WIDTH = 3 * ATTN_WIDTH
PROJ_WIDTH = ATTN_PROJ_WIDTH + 2 * GLA_VALUE_WIDTH + GLA_KEY_WIDTH
N_COL_TILES = PROJ_WIDTH // INPROJ_COLS
N_ATTN_COL_TILES = ATTN_PROJ_WIDTH // INPROJ_COLS
RESIDUE_WIDTH = ATTN_PROJ_WIDTH
GLA_V_COL = ATTN_PROJ_WIDTH
GLA_GATE_COL = GLA_V_COL + GLA_VALUE_WIDTH
GLA_Q_COL = GLA_GATE_COL + GLA_VALUE_WIDTH
ROPE_PARTNER_SHIFT = HEAD_DIM // 2

_NT = (((1,), (1,)), ((), ()))


def _params(semantics):
    return pltpu.CompilerParams(dimension_semantics=semantics, vmem_limit_bytes=VMEM_LIMIT_BYTES)


def _bdot(a, b):
    return jnp.dot(a, b, preferred_element_type=F32)


def _inproj_kernel(x_ref, g_ref, w_ref, wkz_ref, cos_ref, sin_ref,
                   p_ref, p4_ref, p16_ref, kt_ref, z_ref, n_ref, rs_a, rs_b, g4_ref):
    i = pl.program_id(0)
    j = pl.program_id(1)
    rows = x_ref.shape[0]
    heads_per_tile = INPROJ_COLS // HEAD_DIM
    live = i < pl.num_programs(0) - 1

    @pl.when((i == 0) & (j == 0))
    def _():
        rs_b[...] = jnp.zeros_like(rs_b)

    @pl.when((j == 0) & live)
    def _():
        chunk = rows // INPROJ_NORM_CHUNKS
        for c in range(INPROJ_NORM_CHUNKS):
            rs = slice(c * chunk, (c + 1) * chunk)
            x = x_ref[rs, :]
            inv = lax.rsqrt(jnp.mean(x * x, axis=-1, keepdims=True) + EPS)
            n = (x * inv * g_ref[...]).astype(BF16)
            n_ref[rs, :] = n
            kz = lax.dot_general(wkz_ref[...], n, _NT, preferred_element_type=F32)
            kt_ref[:, rs] = kz[0:GLA_KEY_WIDTH, :].astype(BF16)
            z_ref[rs, :] = kz[GLA_KEY_WIDTH:, :].T

    done = jnp.where(j == 0, N_COL_TILES - 1, j - 1)
    scaled = (done < ATTN_WIDTH // INPROJ_COLS) | (done == N_COL_TILES - 1)
    scale = jnp.where(scaled, HEAD_DIM ** -0.5, 1.0).astype(F32)

    def step(cur, prev, regroup):
        cos = cos_ref[...]
        sin = sin_ref[...]
        for h in range(heads_per_tile):
            sl = slice(h * HEAD_DIM, (h + 1) * HEAD_DIM)
            t = prev[h]
            r = (t * cos + pltpu.roll(t, ROPE_PARTNER_SHIFT, 1) * sin) * scale
            p_ref[:, sl] = r.astype(BF16)
            if regroup:
                prev[h] = r
        if regroup:
            per4 = rows // 4
            per16 = rows // 16
            for h in range(heads_per_tile):
                sl = slice(h * HEAD_DIM, (h + 1) * HEAD_DIM)
                for res4 in range(4):
                    grp = prev[h, pl.ds(res4, per4, stride=4), :]
                    p4_ref[0, res4, :, sl] = grp.astype(BF16)
                    g4_ref[h, res4 * per4:(res4 + 1) * per4, :] = grp
                for res4 in range(4):
                    for a in range(4):
                        p16_ref[0, 4 * a + res4, :, sl] = (
                            g4_ref[h, pl.ds(res4 * per4 + a, per16, stride=4), :].astype(BF16))
        acc = _bdot(n_ref[...], w_ref[...])
        for h in range(heads_per_tile):
            cur[h] = acc[:, h * HEAD_DIM:(h + 1) * HEAD_DIM]

    attn_tile = (j >= 1) & (j <= N_ATTN_COL_TILES)
    parity = (i * N_COL_TILES + j) % 2
    work = live | (j == 0)
    for pv, (cur, prev) in enumerate(((rs_a, rs_b), (rs_b, rs_a))):
        for regroup in (True, False):
            @pl.when((parity == pv) & work & (attn_tile if regroup else jnp.logical_not(attn_tile)))
            def _(cur=cur, prev=prev, regroup=regroup):
                step(cur, prev, regroup)


def _inproj(x2, g1, w_main, w_kz, cos_t, sin_t, batch, seq):
    t_rows = x2.shape[0]
    tm = min(INPROJ_ROWS, seq)
    tps = seq // tm
    n_rows = t_rows // tm
    last_col = N_COL_TILES - 1
    row = lambda i: jnp.minimum(i, n_rows - 1)
    n_rotary_tiles = 2 * ATTN_WIDTH // INPROJ_COLS
    table_spec = pl.BlockSpec(
        (tm, HEAD_DIM), lambda i, j: (jnp.where((j >= 1) & (j <= n_rotary_tiles), row(i) % tps, tps), 0))

    def finished(i, j):
        r = jnp.where(j == 0, jnp.maximum(i - 1, 0), row(i))
        c = jnp.where(j == 0, jnp.where(i == 0, 0, last_col), jnp.where(i == n_rows, last_col, j - 1))
        return r, c

    def residue_spec(dil):
        def index(i, j):
            c = jnp.where(i == n_rows, N_ATTN_COL_TILES - 1, jnp.clip(j - 1, 0, N_ATTN_COL_TILES - 1))
            return row(i) // tps, 0, row(i) % tps, c
        return pl.BlockSpec((1, dil, tm // dil, INPROJ_COLS), index)

    slab = pltpu.VMEM((INPROJ_COLS // HEAD_DIM, tm, HEAD_DIM), F32)
    return pl.pallas_call(
        _inproj_kernel,
        name="inproj",
        grid=(n_rows + 1, N_COL_TILES),
        in_specs=[
            pl.BlockSpec((tm, D_MODEL), lambda i, j: (row(i), 0)),
            pl.BlockSpec((1, D_MODEL), lambda i, j: (0, 0)),
            pl.BlockSpec((D_MODEL, INPROJ_COLS), lambda i, j: (0, j)),
            pl.BlockSpec((GLA_KEY_WIDTH + 2 * GATE_RANK, D_MODEL), lambda i, j: (0, 0)),
            table_spec,
            table_spec,
        ],
        out_specs=[
            pl.BlockSpec((tm, INPROJ_COLS), finished),
            residue_spec(4),
            residue_spec(16),
            pl.BlockSpec((GLA_KEY_WIDTH, tm), lambda i, j: (0, row(i))),
            pl.BlockSpec((tm, 2 * GATE_RANK), lambda i, j: (row(i), 0)),
        ],
        out_shape=[
            jax.ShapeDtypeStruct((t_rows, PROJ_WIDTH), BF16),
            jax.ShapeDtypeStruct((batch, 4, seq // 4, RESIDUE_WIDTH), BF16),
            jax.ShapeDtypeStruct((batch, 16, seq // 16, RESIDUE_WIDTH), BF16),
            jax.ShapeDtypeStruct((GLA_KEY_WIDTH, t_rows), BF16),
            jax.ShapeDtypeStruct((t_rows, 2 * GATE_RANK), F32),
        ],
        scratch_shapes=[pltpu.VMEM((tm, D_MODEL), BF16), slab, slab, slab],
        compiler_params=_params(("arbitrary", "arbitrary")),
    )(x2, g1, w_main, w_kz, cos_t, sin_t)


def _attn_kernel(q_ref, kp_ref, kc_ref, kn_ref, vp_ref, vc_ref, vn_ref, o_ref, lse_ref, osc, *, length, dil):
    t = pl.program_id(2)
    tq = q_ref.shape[2]
    win = ATTN_QBLK + 2 * N_SIDE
    n_blk = tq // ATTN_QBLK

    def window(prev_ref, cur_ref, next_ref, res, blk, cols):
        lo, hi = blk * ATTN_QBLK - N_SIDE, blk * ATTN_QBLK + ATTN_QBLK + N_SIDE
        parts = []
        if lo < 0:
            parts.append(prev_ref[0, res, :, cols])
        parts.append(cur_ref[0, res, max(lo, 0):min(hi, tq), cols])
        if hi > tq:
            parts.append(next_ref[0, res, :, cols])
        return parts[0] if len(parts) == 1 else jnp.concatenate(parts, axis=0)

    ones = jnp.ones((win, HEAD_DIM), BF16)

    row = lax.broadcasted_iota(jnp.int32, (ATTN_QBLK, win), 0)
    col = lax.broadcasted_iota(jnp.int32, (ATTN_QBLK, win), 1)
    band = (col >= row) & (col <= row + 2 * N_SIDE)
    lane = lax.broadcasted_iota(jnp.int32, (ATTN_QBLK, LANES), 1)

    for res, blk in [(a, b) for a in range(dil) for b in range(n_blk)]:
        r0 = blk * ATTN_QBLK
        first_key = t * tq + r0 - N_SIDE
        if 0 < blk < n_blk - 1:
            valid = band
        else:
            valid = band & (col >= -first_key) & (col < length - first_key)
        if dil == 1:
            dst = pl.ds(r0, ATTN_QBLK)
        else:
            dst = pl.ds(r0 * dil + res, ATTN_QBLK, stride=dil)
        lse_tile = jnp.zeros((ATTN_QBLK, LANES), F32)
        for h in range(ATTN_HEAD_GROUP):
            hs = slice(h * HEAD_DIM, (h + 1) * HEAD_DIM)
            q = q_ref[0, res, pl.ds(r0, ATTN_QBLK), hs]
            k = window(kp_ref, kc_ref, kn_ref, res, blk, hs)
            v_ones = jnp.concatenate([window(vp_ref, vc_ref, vn_ref, res, blk, hs), ones], axis=1)
            s = lax.dot_general(q, k, _NT, preferred_element_type=F32)
            s = jnp.where(valid, s, -1e30)
            m = jnp.max(s, axis=-1, keepdims=True)
            p = jnp.exp(s - m)
            o_den = _bdot(p.astype(BF16), v_ones)
            den = o_den[:, HEAD_DIM:]
            osc[h, dst, :] = o_den[:, :HEAD_DIM] / den
            lse_tile = jnp.where(lane == h, m + jnp.log(den), lse_tile)
        lse_ref[dst, :] = lse_tile

    for h in range(ATTN_HEAD_GROUP):
        o_ref[:, h * HEAD_DIM:(h + 1) * HEAD_DIM] = osc[h].astype(o_ref.dtype)


def _attention_branch(proj_res, batch, seq, dil):
    length = seq // dil
    tn = min(ATTN_TILE, seq)
    tq = tn // dil
    n_tiles = seq // tn
    halo_per_tile = tq // N_SIDE
    n_halo_blocks = length // N_SIDE
    gw = ATTN_GROUP_WIDTH
    part_blocks = ATTN_WIDTH // gw

    def cur(part):
        return pl.BlockSpec((1, dil, tq, gw), lambda b, g, t: (b, 0, t, part * part_blocks + g))

    def prev(part):
        return pl.BlockSpec((1, dil, N_SIDE, gw),
                            lambda b, g, t: (b, 0, jnp.maximum(t * halo_per_tile - 1, 0), part * part_blocks + g))

    def nxt(part):
        return pl.BlockSpec((1, dil, N_SIDE, gw),
                            lambda b, g, t: (b, 0, jnp.minimum((t + 1) * halo_per_tile, n_halo_blocks - 1),
                                             part * part_blocks + g))

    return pl.pallas_call(
        functools.partial(_attn_kernel, length=length, dil=dil),
        name=f"attn_d{dil}",
        grid=(batch, ATTN_GROUPS, n_tiles),
        in_specs=[cur(0), prev(1), cur(1), nxt(1), prev(2), cur(2), nxt(2)],
        out_specs=[
            pl.BlockSpec((tn, gw), lambda b, g, t: (b * n_tiles + t, g)),
            pl.BlockSpec((tn, LANES), lambda b, g, t: (b * n_tiles + t, g)),
        ],
        out_shape=[
            jax.ShapeDtypeStruct((batch * seq, ATTN_WIDTH), BF16),
            jax.ShapeDtypeStruct((batch * seq, ATTN_GROUPS * LANES), F32),
        ],
        scratch_shapes=[pltpu.VMEM((ATTN_HEAD_GROUP, tn, HEAD_DIM), F32)],
        compiler_params=_params(("parallel", "parallel", "parallel")),
    )(proj_res, proj_res, proj_res, proj_res, proj_res, proj_res, proj_res)


def _split2(a):
    hi = a.astype(BF16)
    lo = (a - hi.astype(F32)).astype(BF16)
    return hi, lo


def _dot_f32(a, b):
    a_hi, a_lo = _split2(a)
    b_hi, b_lo = _split2(b)
    return _bdot(a_hi, b_hi) + (_bdot(a_hi, b_lo) + _bdot(a_lo, b_hi))


def _log_sigmoid(x):
    return jnp.minimum(x, 0.0) - jnp.log(1.0 + jnp.exp(-jnp.abs(x)))


class _Direction:
    def __init__(self, refs, forward):
        self.q_ref, self.kt_ref, self.v_ref, z_ref, up_ref, bias_ref, self.o_ref, self.state_ref = refs
        c = GLA_CHUNK
        r = lax.broadcasted_iota(jnp.int32, (GLA_PAIR, GLA_PAIR), 0)
        cc = lax.broadcasted_iota(jnp.int32, (GLA_PAIR, GLA_PAIR), 1)
        same = jnp.where(r < c, 0, 1) == jnp.where(cc < c, 0, 1)
        lower = same & (r >= cc)
        upper = same & (r <= cc)
        self.row_low = r < c
        self.col_low = cc < c
        if forward:
            left, self.causal = lower, lower
            self.ref_i, self.last_i = c // 2 - 1, c - 1
            self.cross = (r >= c) & (cc < c)
            self.i_first, self.i_second = self.last_i, c + self.last_i
            self.second_rows, self.first_cols = r >= c, cc < c
        else:
            left, self.causal = upper, upper
            self.ref_i, self.last_i = c // 2, 0
            self.cross = (r < c) & (cc >= c)
            self.i_first, self.i_second = c + self.last_i, self.last_i
            self.second_rows, self.first_cols = r < c, cc >= c
        left_m = jnp.where(left, 1.0, 0.0).astype(BF16)
        self.left2_m = jnp.concatenate([left_m, left_m], axis=1)
        inv_norm = 1.0 / GATE_NORMALIZER
        self.g = _log_sigmoid(_dot_f32(z_ref[...], up_ref[...]) + bias_ref[...]) * inv_norm


def _gla_kernel(qf_ref, ktf_ref, vf_ref, zf_ref, qb_ref, ktb_ref, vb_ref, zb_ref,
                upf_ref, bf_ref, upb_ref, bb_ref, of_ref, ob_ref, sf_ref, sb_ref):
    @pl.when(pl.program_id(2) == 0)
    def _():
        sf_ref[...] = jnp.zeros_like(sf_ref)
        sb_ref[...] = jnp.zeros_like(sb_ref)

    c = GLA_CHUNK
    ts = qf_ref.shape[0]
    n_pairs = ts // GLA_PAIR
    fwd = _Direction((qf_ref, ktf_ref, vf_ref, zf_ref, upf_ref, bf_ref, of_ref, sf_ref), True)
    bwd = _Direction((qb_ref, ktb_ref, vb_ref, zb_ref, upb_ref, bb_ref, ob_ref, sb_ref), False)
    jobs = []
    for s in range(n_pairs):
        jobs.append((fwd, slice(s * GLA_PAIR, (s + 1) * GLA_PAIR)))
        jobs.append((bwd, slice((n_pairs - 1 - s) * GLA_PAIR, (n_pairs - s) * GLA_PAIR)))

    cums = []
    for d, rows in jobs:
        g_hi, g_lo = _split2(d.g[rows])
        b = _bdot(d.left2_m, jnp.concatenate([g_hi, g_lo], axis=0))
        cums.append((b, b.T))

    ops = []
    for (d, rows), (b, bt) in zip(jobs, cums):
        def rowpick(a, i):
            return jnp.where(d.row_low, a[i:i + 1, :], a[c + i:c + i + 1, :])

        def colpick(a, i):
            return jnp.where(d.col_low, a[:, i:i + 1], a[:, c + i:c + i + 1])

        q = d.q_ref[rows, :].astype(F32)
        kt = d.kt_ref[:, rows].astype(F32)
        qe = (q * jnp.exp(b - rowpick(b, d.ref_i))).astype(BF16)
        q_in = q * jnp.exp(b)
        ket = (kt * jnp.exp(colpick(bt, d.ref_i) - bt)).astype(BF16)
        kst = kt * jnp.exp(colpick(bt, d.last_i) - bt)
        dec_first_row = jnp.exp(b[d.i_first:d.i_first + 1, :])
        dec_second_col = jnp.exp(bt[:, d.i_second:d.i_second + 1])
        dec_total_col = jnp.exp(bt[:, d.i_first:d.i_first + 1] + bt[:, d.i_second:d.i_second + 1])
        q_state = (q_in * jnp.where(d.second_rows, dec_first_row, 1.0)).astype(BF16)
        k_state = (kst * jnp.where(d.first_cols, dec_second_col, 1.0)).astype(BF16)
        ops.append((qe, ket, q_in.astype(BF16), kst.astype(BF16), q_state, k_state, dec_total_col))

    scores = []
    for qe, ket, q_in, kst, _, _, _ in ops:
        scores.append((_bdot(qe, ket), _bdot(q_in, kst)))

    locals_ = []
    for (d, rows), (a_diag, a_cross), op in zip(jobs, scores, ops):
        a = jnp.where(d.causal, a_diag, jnp.where(d.cross, a_cross, 0.0)).astype(BF16)
        v = d.v_ref[rows, :]
        locals_.append((jnp.concatenate([a, op[4]], axis=1), v, _bdot(op[5], v)))

    state = {id(fwd): sf_ref[...], id(bwd): sb_ref[...]}
    for (d, rows), (a_q, v, upd), op in zip(jobs, locals_, ops):
        s = state[id(d)]
        o = _bdot(a_q, jnp.concatenate([v, s.astype(BF16)], axis=0))
        d.o_ref[rows, :] = o.astype(d.o_ref.dtype)
        state[id(d)] = op[6] * s + upd
    sf_ref[...] = state[id(fwd)]
    sb_ref[...] = state[id(bwd)]


def _gla(proj, kt, z, gf_up, gf_b, gb_up, gb_b, batch, seq):
    t_rows = batch * seq
    ts = min(GLA_ROWS, seq)
    n_s = seq // ts
    q_col0 = GLA_Q_COL // GLA_DK
    v_col0 = GLA_V_COL // GLA_DV

    zero = jnp.zeros((GATE_RANK, GLA_KEY_WIDTH), F32)
    upf = jnp.concatenate([gf_up, zero], axis=0)
    upb = jnp.concatenate([zero, gb_up], axis=0)
    bf = gf_b.reshape(1, GLA_KEY_WIDTH)
    bb = gb_b.reshape(1, GLA_KEY_WIDTH)

    def tile_specs(tile):
        return [
            pl.BlockSpec((ts, GLA_DK), lambda b, h, i: (tile(b, i), q_col0 + h)),
            pl.BlockSpec((GLA_DK, ts), lambda b, h, i: (h, tile(b, i))),
            pl.BlockSpec((ts, GLA_DV), lambda b, h, i: (tile(b, i), v_col0 + h)),
            pl.BlockSpec((ts, 2 * GATE_RANK), lambda b, h, i: (tile(b, i), 0)),
        ]

    fwd_tile = lambda b, i: b * n_s + i
    bwd_tile = lambda b, i: b * n_s + (n_s - 1 - i)
    gate_specs = [
        pl.BlockSpec((2 * GATE_RANK, GLA_DK), lambda b, h, i: (0, h)),
        pl.BlockSpec((1, GLA_DK), lambda b, h, i: (0, h)),
    ]
    return pl.pallas_call(
        _gla_kernel,
        name="gla",
        grid=(batch, GLA_HEADS, n_s),
        in_specs=tile_specs(fwd_tile) + tile_specs(bwd_tile) + gate_specs + gate_specs,
        out_specs=[
            pl.BlockSpec((ts, GLA_DV), lambda b, h, i: (fwd_tile(b, i), h)),
            pl.BlockSpec((ts, GLA_DV), lambda b, h, i: (bwd_tile(b, i), h)),
        ],
        out_shape=[jax.ShapeDtypeStruct((t_rows, GLA_VALUE_WIDTH), BF16)] * 2,
        scratch_shapes=[pltpu.VMEM((GLA_DK, GLA_DV), F32), pltpu.VMEM((GLA_DK, GLA_DV), F32)],
        compiler_params=_params(("parallel", "parallel", "arbitrary")),
    )(proj, kt, proj, z, proj, kt, proj, z, upf, bf, upb, bb)


def _outproj_kernel(o1_ref, o2_ref, o3_ref, l1_ref, l2_ref, l3_ref, of_ref, ob_ref, gr_ref, x_ref,
                    ag_ref, gg_ref, w_ref, h_ref, ao_ref, cat_a, cat_b):
    i = pl.program_id(0)
    rows = x_ref.shape[0]

    @pl.when(i == 0)
    def _():
        cat_b[...] = jnp.zeros_like(cat_b)

    def step(cur, prev):
        l1, l2, l3 = l1_ref[...], l2_ref[...], l3_ref[...]
        m = jnp.maximum(jnp.maximum(l1, l2), l3)
        e1, e2, e3 = jnp.exp(l1 - m), jnp.exp(l2 - m), jnp.exp(l3 - m)
        inv = 1.0 / (e1 + e2 + e3)
        w1, w2, w3 = e1 * inv, e2 * inv, e3 * inv

        ssq = jnp.zeros((rows, 1), F32)
        for h in range(ATTN_HEADS):
            hs = slice(h * HEAD_DIM, (h + 1) * HEAD_DIM)
            lc = (h // ATTN_HEAD_GROUP) * LANES + h % ATTN_HEAD_GROUP
            ao = (w1[:, lc:lc + 1] * o1_ref[:, hs].astype(F32) + w2[:, lc:lc + 1] * o2_ref[:, hs].astype(F32)
                  + w3[:, lc:lc + 1] * o3_ref[:, hs].astype(F32))
            ssq = ssq + jnp.sum(ao * ao, axis=-1, keepdims=True)
            ao_ref[:, hs] = ao
        inv_rms = lax.rsqrt(ssq * (1.0 / ATTN_WIDTH) + EPS)
        cur[:, 0:ATTN_WIDTH] = (ao_ref[...] * inv_rms * ag_ref[...]).astype(BF16)

        for h in range(GLA_HEADS):
            hs = slice(h * GLA_DV, (h + 1) * GLA_DV)
            o = of_ref[:, hs].astype(F32) + ob_ref[:, hs].astype(F32)
            y = o * lax.rsqrt(jnp.mean(o * o, axis=-1, keepdims=True) + EPS) * gg_ref[...]
            gate = gr_ref[:, hs].astype(F32)
            y = y * (gate / (1.0 + jnp.exp(-gate)))
            cur[:, ATTN_WIDTH + h * GLA_DV:ATTN_WIDTH + (h + 1) * GLA_DV] = y.astype(BF16)

        h_ref[...] = x_ref[...] + _bdot(prev[...], w_ref[...])

    @pl.when(i % 2 == 0)
    def _():
        step(cat_a, cat_b)

    @pl.when(i % 2 == 1)
    def _():
        step(cat_b, cat_a)


def _outproj(branches, o_f, o_b, proj, x2, attn_g, gla_g, w_out):
    t_rows = x2.shape[0]
    tm = OUTPROJ_ROWS
    n_tiles = t_rows // tm
    mixed = lambda i: jnp.minimum(i, n_tiles - 1)
    projected = lambda i: jnp.maximum(i - 1, 0)
    mix = lambda width: pl.BlockSpec((tm, width), lambda i: (mixed(i), 0))
    (o1, l1), (o2, l2), (o3, l3) = branches
    lse_w = ATTN_GROUPS * LANES
    cat = pltpu.VMEM((tm, D_MODEL), BF16)
    return pl.pallas_call(
        _outproj_kernel,
        name="outproj",
        grid=(n_tiles + 1,),
        in_specs=[
            mix(ATTN_WIDTH), mix(ATTN_WIDTH), mix(ATTN_WIDTH), mix(lse_w), mix(lse_w), mix(lse_w),
            mix(GLA_VALUE_WIDTH), mix(GLA_VALUE_WIDTH),
            pl.BlockSpec((tm, GLA_VALUE_WIDTH), lambda i: (mixed(i), GLA_GATE_COL // GLA_VALUE_WIDTH)),
            pl.BlockSpec((tm, D_MODEL), lambda i: (projected(i), 0)),
            pl.BlockSpec((1, ATTN_WIDTH), lambda i: (0, 0)),
            pl.BlockSpec((1, GLA_DV), lambda i: (0, 0)),
            pl.BlockSpec((D_MODEL, D_MODEL), lambda i: (0, 0), pipeline_mode=pl.Buffered(1)),
        ],
        out_specs=pl.BlockSpec((tm, D_MODEL), lambda i: (projected(i), 0)),
        out_shape=jax.ShapeDtypeStruct((t_rows, D_MODEL), F32),
        scratch_shapes=[pltpu.VMEM((tm, ATTN_WIDTH), F32), cat, cat],
        compiler_params=_params(("arbitrary",)),
    )(o1, o2, o3, l1, l2, l3, o_f, o_b, proj, x2, attn_g, gla_g, w_out)


def _ffn_kernel(h_ref, hp_ref, hn_ref, g2_ref, wg_ref, wu_ref, cw_ref, cb_ref, wd_ref, gf_ref,
                out_ref, n_ref, *, tiles_per_seq, final_norm):
    i = pl.program_id(0)
    j = pl.program_id(1)
    tm = h_ref.shape[0]
    halo = FFN_HALO

    def norm(x):
        return (x * lax.rsqrt(jnp.mean(x * x, axis=-1, keepdims=True) + EPS) * g2_ref[...]).astype(BF16)

    @pl.when(j == 0)
    def _():
        n_ref[0:halo] = norm(hp_ref[...])
        n_ref[halo:halo + tm] = norm(h_ref[...])
        n_ref[halo + tm:] = norm(hn_ref[...])
        out_ref[...] = h_ref[...]

    gate = _bdot(n_ref[...], wg_ref[...])
    ext = tm + 2 * halo
    row = lax.broadcasted_iota(jnp.int32, (tm, 1), 0)
    seq_tile = i % tiles_per_seq
    g_prev = jnp.where((row == 0) & (seq_tile == 0), 0.0, pltpu.roll(gate, 1, 0)[halo:halo + tm, :])
    g_next = jnp.where((row == tm - 1) & (seq_tile == tiles_per_seq - 1), 0.0,
                       pltpu.roll(gate, ext - 1, 0)[halo:halo + tm, :])
    g_mid = gate[halo:halo + tm, :]
    cw = cw_ref[...]
    conv = g_prev * cw[0:1, :] + g_mid * cw[1:2, :] + g_next * cw[2:3, :] + cb_ref[...]
    up = _bdot(n_ref[halo:halo + tm, :], wu_ref[...])
    act = (conv / (1.0 + jnp.exp(-conv))) * up
    out_ref[...] += _bdot(act.astype(BF16), wd_ref[...])

    if final_norm:
        @pl.when(j == pl.num_programs(1) - 1)
        def _():
            y = out_ref[...]
            out_ref[...] = y * lax.rsqrt(jnp.mean(y * y, axis=-1, keepdims=True) + EPS) * gf_ref[...]


def _ffn(h2, g2, wg, wu, conv_w, conv_b, wd, gfinal, seq, final_norm):
    t_rows = h2.shape[0]
    tm = min(FFN_ROWS, seq)
    tf = FFN_COLS
    halo_blocks_per_tile = tm // FFN_HALO
    n_halo_blocks = t_rows // FFN_HALO
    return pl.pallas_call(
        functools.partial(_ffn_kernel, tiles_per_seq=seq // tm, final_norm=final_norm),
        name="ffn",
        grid=(t_rows // tm, D_FF // tf),
        in_specs=[
            pl.BlockSpec((tm, D_MODEL), lambda i, j: (i, 0)),
            pl.BlockSpec((FFN_HALO, D_MODEL), lambda i, j: (jnp.maximum(i * halo_blocks_per_tile - 1, 0), 0)),
            pl.BlockSpec((FFN_HALO, D_MODEL),
                         lambda i, j: (jnp.minimum((i + 1) * halo_blocks_per_tile, n_halo_blocks - 1), 0)),
            pl.BlockSpec((1, D_MODEL), lambda i, j: (0, 0)),
            pl.BlockSpec((D_MODEL, tf), lambda i, j: (0, j)),
            pl.BlockSpec((D_MODEL, tf), lambda i, j: (0, j)),
            pl.BlockSpec((CONV_WIDTH, tf), lambda i, j: (0, j)),
            pl.BlockSpec((1, tf), lambda i, j: (0, j)),
            pl.BlockSpec((tf, D_MODEL), lambda i, j: (j, 0)),
            pl.BlockSpec((1, D_MODEL), lambda i, j: (0, 0)),
        ],
        out_specs=pl.BlockSpec((tm, D_MODEL), lambda i, j: (i, 0)),
        out_shape=jax.ShapeDtypeStruct((t_rows, D_MODEL), F32),
        scratch_shapes=[pltpu.VMEM((tm + 2 * FFN_HALO, D_MODEL), BF16)],
        compiler_params=_params(("parallel", "arbitrary")),
    )(h2, h2, h2, g2, wg, wu, conv_w, conv_b, wd, gfinal)


def _rope_tables(seq, identity_rows):
    pos = np.arange(seq, dtype=np.float32)
    inv_freq = np.float32(ROPE_THETA) ** (-np.arange(0, ROPE_DIM, 2, dtype=np.float32) / np.float32(ROPE_DIM))
    ang = (pos[:, None] * inv_freq.astype(np.float32)[None, :]).astype(np.float32)
    cos = np.cos(ang.astype(np.float64)).astype(np.float32)
    sin = np.sin(ang.astype(np.float64)).astype(np.float32)
    gap = ROPE_PARTNER_SHIFT - ROPE_HALF
    ones, zeros = np.ones((seq, gap), np.float32), np.zeros((seq, gap), np.float32)
    cos_t = np.concatenate([cos, ones, cos, ones], axis=1)
    sin_t = np.concatenate([-sin, zeros, sin, zeros], axis=1)
    cos_t = np.concatenate([cos_t, np.ones((identity_rows, HEAD_DIM), np.float32)], axis=0)
    sin_t = np.concatenate([sin_t, np.zeros((identity_rows, HEAD_DIM), np.float32)], axis=0)
    return jnp.asarray(cos_t), jnp.asarray(sin_t)


def _rotary_head_layout(w):
    d_in, width = w.shape
    w = w.reshape(d_in, width // HEAD_DIM, HEAD_DIM)
    split = ROPE_DIM + ROPE_PARTNER_SHIFT - ROPE_HALF
    w = jnp.concatenate([w[..., :ROPE_HALF], w[..., ROPE_DIM:split], w[..., ROPE_HALF:ROPE_DIM], w[..., split:]],
                        axis=-1)
    return w.reshape(d_in, width)


def kernel(x, norm1_g, w_in, gf_up, gf_b, gb_up, gb_b, gla_norm_g, attn_norm_g, w_out, norm2_g, w_gate, w_up,
           conv_w, conv_b, w_down, final_norm_g):
    batch, seq, d_model = x.shape
    depth = norm1_g.shape[0]
    aw, kw, vw = ATTN_WIDTH, GLA_KEY_WIDTH, GLA_VALUE_WIDTH
    assert d_model == D_MODEL and w_in.shape[-1] == 3 * aw + 2 * kw + 2 * vw + 2 * GATE_RANK
    assert all((window // 2) // dil == N_SIDE for window, dil in DILATED_PATTERNS) and DILATIONS == (1, 4, 16)
    assert seq % ATTN_TILE == 0

    cos_t, sin_t = _rope_tables(seq, min(INPROJ_ROWS, seq))
    c_gq, c_gk, c_gv, c_gr, c_z = 3 * aw, 3 * aw + kw, 3 * aw + 2 * kw, 3 * aw + 2 * kw + vw, 3 * aw + 2 * kw + 2 * vw

    h2 = x.reshape(batch * seq, D_MODEL)
    for l in range(depth):
        wi = w_in[l]
        w_main = jnp.concatenate([_rotary_head_layout(wi[:, :2 * aw]), wi[:, 2 * aw:c_gq], wi[:, c_gv:c_gr],
                                  wi[:, c_gr:c_z], wi[:, c_gq:c_gk]], axis=1).astype(BF16)
        w_kz = jnp.concatenate([wi[:, c_gk:c_gv], wi[:, c_z:]], axis=1).T.astype(BF16)

        proj, proj4, proj16, kt, z = _inproj(
            h2, norm1_g[l].reshape(1, D_MODEL), w_main, w_kz, cos_t, sin_t, batch, seq)
        by_residue = {1: proj.reshape(batch, 1, seq, PROJ_WIDTH), 4: proj4, 16: proj16}
        branches = [_attention_branch(by_residue[dil], batch, seq, dil) for dil in DILATIONS]
        o_f, o_b = _gla(proj, kt, z, gf_up[l], gf_b[l], gb_up[l], gb_b[l], batch, seq)
        h2 = _outproj(branches, o_f, o_b, proj, h2, attn_norm_g[l].reshape(1, ATTN_WIDTH),
                      gla_norm_g[l].reshape(1, GLA_DV), w_out[l].astype(BF16))
        h2 = _ffn(h2, norm2_g[l].reshape(1, D_MODEL), w_gate[l].astype(BF16), w_up[l].astype(BF16), conv_w[l],
                  conv_b[l].reshape(1, D_FF), w_down[l].astype(BF16), final_norm_g.reshape(1, D_MODEL), seq,
                  final_norm=(l == depth - 1))
    return h2.reshape(batch, seq, D_MODEL)
```

```python
import functools

import jax
import jax.numpy as jnp
import numpy as np
from jax import lax
from jax.experimental import pallas as pl
from jax.experimental.pallas import tpu as pltpu

F32 = jnp.float32
BF16 = jnp.bfloat16

D_MODEL = 2048
ATTN_WIDTH = 1024
HEAD_DIM = 128
ATTN_HEADS = ATTN_WIDTH // HEAD_DIM
DILATED_PATTERNS = ((128, 1), (512, 4), (2048, 16))
DILATIONS = tuple(d for _, d in DILATED_PATTERNS)
N_SIDE = 64
ROPE_THETA = 500000.0
ROPE_DIM = HEAD_DIM // 4
ROPE_HALF = ROPE_DIM // 2
GLA_HEADS = 4
GLA_DK = 128
GLA_DV = 256
GLA_KEY_WIDTH = GLA_HEADS * GLA_DK
GLA_VALUE_WIDTH = GLA_HEADS * GLA_DV
GATE_RANK = 16
GATE_NORMALIZER = 16.0
GLA_CHUNK = 64
D_FF = 5632
CONV_WIDTH = 3
EPS = 1e-6

LANES = 128
BF16_SUBLANES = 16
VMEM_LIMIT_BYTES = 60 * 1024 * 1024

INPROJ_ROWS = 1024
INPROJ_COLS = 512
INPROJ_NORM_CHUNKS = 4
ATTN_TILE = 4096
ATTN_QBLK = 128
ATTN_HEAD_GROUP = 4
ATTN_GROUP_WIDTH = ATTN_HEAD_GROUP * HEAD_DIM
ATTN_GROUPS = ATTN_HEADS // ATTN_HEAD_GROUP
GLA_ROWS = 2048
GLA_PAIR = 2 * GLA_CHUNK
OUTPROJ_ROWS = 512
FFN_ROWS = 1024
FFN_COLS = 512
FFN_HALO = BF16_SUBLANES

ATTN_PROJ_WIDTH = 3 * ATTN_WIDTH
PROJ_WIDTH = ATTN_PROJ_WIDTH + 2 * GLA_VALUE_WIDTH + GLA_KEY_WIDTH
N_COL_TILES = PROJ_WIDTH // INPROJ_COLS
N_ATTN_COL_TILES = ATTN_PROJ_WIDTH // INPROJ_COLS
RESIDUE_WIDTH = ATTN_PROJ_WIDTH
GLA_V_COL = ATTN_PROJ_WIDTH
GLA_GATE_COL = GLA_V_COL + GLA_VALUE_WIDTH
GLA_Q_COL = GLA_GATE_COL + GLA_VALUE_WIDTH
ROPE_PARTNER_SHIFT = HEAD_DIM // 2

_NT = (((1,), (1,)), ((), ()))


def _params(semantics):
    return pltpu.CompilerParams(dimension_semantics=semantics, vmem_limit_bytes=VMEM_LIMIT_BYTES)


def _bdot(a, b):
    return jnp.dot(a, b, preferred_element_type=F32)


def _inproj_kernel(x_ref, g_ref, w_ref, wkz_ref, cos_ref, sin_ref,
                   p_ref, p4_ref, p16_ref, kt_ref, z_ref, n_ref, rs_a, rs_b, g4_ref):
    i = pl.program_id(0)
    j = pl.program_id(1)
    rows = x_ref.shape[0]
    heads_per_tile = INPROJ_COLS // HEAD_DIM
    live = i < pl.num_programs(0) - 1

    @pl.when((i == 0) & (j == 0))
    def _():
        rs_b[...] = jnp.zeros_like(rs_b)

    @pl.when((j == 0) & live)
    def _():
        chunk = rows // INPROJ_NORM_CHUNKS
        for c in range(INPROJ_NORM_CHUNKS):
            rs = slice(c * chunk, (c + 1) * chunk)
            x = x_ref[rs, :]
            inv = lax.rsqrt(jnp.mean(x * x, axis=-1, keepdims=True) + EPS)
            n = (x * inv * g_ref[...]).astype(BF16)
            n_ref[rs, :] = n
            kz = lax.dot_general(wkz_ref[...], n, _NT, preferred_element_type=F32)
            kt_ref[:, rs] = kz[0:GLA_KEY_WIDTH, :].astype(BF16)
            z_ref[rs, :] = kz[GLA_KEY_WIDTH:, :].T

    done = jnp.where(j == 0, N_COL_TILES - 1, j - 1)
    scaled = (done < ATTN_WIDTH // INPROJ_COLS) | (done == N_COL_TILES - 1)
    scale = jnp.where(scaled, HEAD_DIM ** -0.5, 1.0).astype(F32)

    def step(cur, prev, regroup):
        cos = cos_ref[...]
        sin = sin_ref[...]
        for h in range(heads_per_tile):
            sl = slice(h * HEAD_DIM, (h + 1) * HEAD_DIM)
            t = prev[h]
            r = (t * cos + pltpu.roll(t, ROPE_PARTNER_SHIFT, 1) * sin) * scale
            p_ref[:, sl] = r.astype(BF16)
            if regroup:
                prev[h] = r
        if regroup:
            per4 = rows // 4
            per16 = rows // 16
            for h in range(heads_per_tile):
                sl = slice(h * HEAD_DIM, (h + 1) * HEAD_DIM)
                for res4 in range(4):
                    grp = prev[h, pl.ds(res4, per4, stride=4), :]
                    p4_ref[0, res4, :, sl] = grp.astype(BF16)
                    g4_ref[h, res4 * per4:(res4 + 1) * per4, :] = grp
                for res4 in range(4):
                    for a in range(4):
                        p16_ref[0, 4 * a + res4, :, sl] = (
                            g4_ref[h, pl.ds(res4 * per4 + a, per16, stride=4), :].astype(BF16))
        acc = _bdot(n_ref[...], w_ref[...])
        for h in range(heads_per_tile):
            cur[h] = acc[:, h * HEAD_DIM:(h + 1) * HEAD_DIM]

    attn_tile = (j >= 1) & (j <= N_ATTN_COL_TILES)
    parity = (i * N_COL_TILES + j) % 2
    work = live | (j == 0)
    for pv, (cur, prev) in enumerate(((rs_a, rs_b), (rs_b, rs_a))):
        for regroup in (True, False):
            @pl.when((parity == pv) & work & (attn_tile if regroup else jnp.logical_not(attn_tile)))
            def _(cur=cur, prev=prev, regroup=regroup):
                step(cur, prev, regroup)


def _inproj(x2, g1, w_main, w_kz, cos_t, sin_t, batch, seq):
    t_rows = x2.shape[0]
    tm = min(INPROJ_ROWS, seq)
    tps = seq // tm
    n_rows = t_rows // tm
    last_col = N_COL_TILES - 1
    row = lambda i: jnp.minimum(i, n_rows - 1)
    n_rotary_tiles = 2 * ATTN_WIDTH // INPROJ_COLS
    table_spec = pl.BlockSpec(
        (tm, HEAD_DIM), lambda i, j: (jnp.where((j >= 1) & (j <= n_rotary_tiles), row(i) % tps, tps), 0))

    def finished(i, j):
        r = jnp.where(j == 0, jnp.maximum(i - 1, 0), row(i))
        c = jnp.where(j == 0, jnp.where(i == 0, 0, last_col), jnp.where(i == n_rows, last_col, j - 1))
        return r, c

    def residue_spec(dil):
        def index(i, j):
            c = jnp.where(i == n_rows, N_ATTN_COL_TILES - 1, jnp.clip(j - 1, 0, N_ATTN_COL_TILES - 1))
            return row(i) // tps, 0, row(i) % tps, c
        return pl.BlockSpec((1, dil, tm // dil, INPROJ_COLS), index)

    slab = pltpu.VMEM((INPROJ_COLS // HEAD_DIM, tm, HEAD_DIM), F32)
    return pl.pallas_call(
        _inproj_kernel,
        name="inproj",
        grid=(n_rows + 1, N_COL_TILES),
        in_specs=[
            pl.BlockSpec((tm, D_MODEL), lambda i, j: (row(i), 0)),
            pl.BlockSpec((1, D_MODEL), lambda i, j: (0, 0)),
            pl.BlockSpec((D_MODEL, INPROJ_COLS), lambda i, j: (0, j)),
            pl.BlockSpec((GLA_KEY_WIDTH + 2 * GATE_RANK, D_MODEL), lambda i, j: (0, 0)),
            table_spec,
            table_spec,
        ],
        out_specs=[
            pl.BlockSpec((tm, INPROJ_COLS), finished),
            residue_spec(4),
            residue_spec(16),
            pl.BlockSpec((GLA_KEY_WIDTH, tm), lambda i, j: (0, row(i))),
            pl.BlockSpec((tm, 2 * GATE_RANK), lambda i, j: (row(i), 0)),
        ],
        out_shape=[
            jax.ShapeDtypeStruct((t_rows, PROJ_WIDTH), BF16),
            jax.ShapeDtypeStruct((batch, 4, seq // 4, RESIDUE_WIDTH), BF16),
            jax.ShapeDtypeStruct((batch, 16, seq // 16, RESIDUE_WIDTH), BF16),
            jax.ShapeDtypeStruct((GLA_KEY_WIDTH, t_rows), BF16),
            jax.ShapeDtypeStruct((t_rows, 2 * GATE_RANK), F32),
        ],
        scratch_shapes=[pltpu.VMEM((tm, D_MODEL), BF16), slab, slab, slab],
        compiler_params=_params(("arbitrary", "arbitrary")),
    )(x2, g1, w_main, w_kz, cos_t, sin_t)


def _attn_kernel(q_ref, kp_ref, kc_ref, kn_ref, vp_ref, vc_ref, vn_ref, o_ref, lse_ref, osc, *, length, dil):
    t = pl.program_id(2)
    tq = q_ref.shape[2]
    win = ATTN_QBLK + 2 * N_SIDE
    n_blk = tq // ATTN_QBLK

    def window(prev_ref, cur_ref, next_ref, res, blk, cols):
        lo, hi = blk * ATTN_QBLK - N_SIDE, blk * ATTN_QBLK + ATTN_QBLK + N_SIDE
        parts = []
        if lo < 0:
            parts.append(prev_ref[0, res, :, cols])
        parts.append(cur_ref[0, res, max(lo, 0):min(hi, tq), cols])
        if hi > tq:
            parts.append(next_ref[0, res, :, cols])
        return parts[0] if len(parts) == 1 else jnp.concatenate(parts, axis=0)

    ones = jnp.ones((win, HEAD_DIM), BF16)

    row = lax.broadcasted_iota(jnp.int32, (ATTN_QBLK, win), 0)
    col = lax.broadcasted_iota(jnp.int32, (ATTN_QBLK, win), 1)
    band = (col >= row) & (col <= row + 2 * N_SIDE)
    lane = lax.broadcasted_iota(jnp.int32, (ATTN_QBLK, LANES), 1)

    for res, blk in [(a, b) for a in range(dil) for b in range(n_blk)]:
        r0 = blk * ATTN_QBLK
        first_key = t * tq + r0 - N_SIDE
        if 0 < blk < n_blk - 1:
            valid = band
        else:
            valid = band & (col >= -first_key) & (col < length - first_key)
        if dil == 1:
            dst = pl.ds(r0, ATTN_QBLK)
        else:
            dst = pl.ds(r0 * dil + res, ATTN_QBLK, stride=dil)
        lse_tile = jnp.zeros((ATTN_QBLK, LANES), F32)
        for h in range(ATTN_HEAD_GROUP):
            hs = slice(h * HEAD_DIM, (h + 1) * HEAD_DIM)
            q = q_ref[0, res, pl.ds(r0, ATTN_QBLK), hs]
            k = window(kp_ref, kc_ref, kn_ref, res, blk, hs)
            v_ones = jnp.concatenate([window(vp_ref, vc_ref, vn_ref, res, blk, hs), ones], axis=1)
            s = lax.dot_general(q, k, _NT, preferred_element_type=F32)
            s = jnp.where(valid, s, -1e30)
            m = jnp.max(s, axis=-1, keepdims=True)
            p = jnp.exp(s - m)
            o_den = _bdot(p.astype(BF16), v_ones)
            den = o_den[:, HEAD_DIM:]
            osc[h, dst, :] = o_den[:, :HEAD_DIM] / den
            lse_tile = jnp.where(lane == h, m + jnp.log(den), lse_tile)
        lse_ref[dst, :] = lse_tile

    for h in range(ATTN_HEAD_GROUP):
        o_ref[:, h * HEAD_DIM:(h + 1) * HEAD_DIM] = osc[h].astype(o_ref.dtype)


def _attention_branch(proj_res, batch, seq, dil):
    length = seq // dil
    tn = min(ATTN_TILE, seq)
    tq = tn // dil
    n_tiles = seq // tn
    halo_per_tile = tq // N_SIDE
    n_halo_blocks = length // N_SIDE
    gw = ATTN_GROUP_WIDTH
    part_blocks = ATTN_WIDTH // gw

    def cur(part):
        return pl.BlockSpec((1, dil, tq, gw), lambda b, g, t: (b, 0, t, part * part_blocks + g))

    def prev(part):
        return pl.BlockSpec((1, dil, N_SIDE, gw),
                            lambda b, g, t: (b, 0, jnp.maximum(t * halo_per_tile - 1, 0), part * part_blocks + g))

    def nxt(part):
        return pl.BlockSpec((1, dil, N_SIDE, gw),
                            lambda b, g, t: (b, 0, jnp.minimum((t + 1) * halo_per_tile, n_halo_blocks - 1),
                                             part * part_blocks + g))

    return pl.pallas_call(
        functools.partial(_attn_kernel, length=length, dil=dil),
        name=f"attn_d{dil}",
        grid=(batch, ATTN_GROUPS, n_tiles),
        in_specs=[cur(0), prev(1), cur(1), nxt(1), prev(2), cur(2), nxt(2)],
        out_specs=[
            pl.BlockSpec((tn, gw), lambda b, g, t: (b * n_tiles + t, g)),
            pl.BlockSpec((tn, LANES), lambda b, g, t: (b * n_tiles + t, g)),
        ],
        out_shape=[
            jax.ShapeDtypeStruct((batch * seq, ATTN_WIDTH), BF16),
            jax.ShapeDtypeStruct((batch * seq, ATTN_GROUPS * LANES), F32),
        ],
        scratch_shapes=[pltpu.VMEM((ATTN_HEAD_GROUP, tn, HEAD_DIM), F32)],
        compiler_params=_params(("parallel", "parallel", "parallel")),
    )(proj_res, proj_res, proj_res, proj_res, proj_res, proj_res, proj_res)


def _split2(a):
    hi = a.astype(BF16)
    lo = (a - hi.astype(F32)).astype(BF16)
    return hi, lo


def _dot_f32(a, b):
    a_hi, a_lo = _split2(a)
    b_hi, b_lo = _split2(b)
    return _bdot(a_hi, b_hi) + (_bdot(a_hi, b_lo) + _bdot(a_lo, b_hi))


def _log_sigmoid(x):
    return jnp.minimum(x, 0.0) - jnp.log(1.0 + jnp.exp(-jnp.abs(x)))


class _Direction:
    def __init__(self, refs, forward):
        self.q_ref, self.kt_ref, self.v_ref, z_ref, up_ref, bias_ref, self.o_ref, self.state_ref = refs
        c = GLA_CHUNK
        r = lax.broadcasted_iota(jnp.int32, (GLA_PAIR, GLA_PAIR), 0)
        cc = lax.broadcasted_iota(jnp.int32, (GLA_PAIR, GLA_PAIR), 1)
        same = jnp.where(r < c, 0, 1) == jnp.where(cc < c, 0, 1)
        lower = same & (r >= cc)
        upper = same & (r <= cc)
        self.row_low = r < c
        self.col_low = cc < c
        if forward:
            left, self.causal = lower, lower
            self.ref_i, self.last_i = c // 2 - 1, c - 1
            self.cross = (r >= c) & (cc < c)
            self.i_first, self.i_second = self.last_i, c + self.last_i
            self.second_rows, self.first_cols = r >= c, cc < c
        else:
            left, self.causal = upper, upper
            self.ref_i, self.last_i = c // 2, 0
            self.cross = (r < c) & (cc >= c)
            self.i_first, self.i_second = c + self.last_i, self.last_i
            self.second_rows, self.first_cols = r < c, cc >= c
        left_m = jnp.where(left, 1.0, 0.0).astype(BF16)
        self.left2_m = jnp.concatenate([left_m, left_m], axis=1)
        inv_norm = 1.0 / GATE_NORMALIZER
        self.g = _log_sigmoid(_dot_f32(z_ref[...], up_ref[...]) + bias_ref[...]) * inv_norm


def _gla_kernel(qf_ref, ktf_ref, vf_ref, zf_ref, qb_ref, ktb_ref, vb_ref, zb_ref,
                upf_ref, bf_ref, upb_ref, bb_ref, of_ref, ob_ref, sf_ref, sb_ref):
    @pl.when(pl.program_id(2) == 0)
    def _():
        sf_ref[...] = jnp.zeros_like(sf_ref)
        sb_ref[...] = jnp.zeros_like(sb_ref)

    c = GLA_CHUNK
    ts = qf_ref.shape[0]
    n_pairs = ts // GLA_PAIR
    fwd = _Direction((qf_ref, ktf_ref, vf_ref, zf_ref, upf_ref, bf_ref, of_ref, sf_ref), True)
    bwd = _Direction((qb_ref, ktb_ref, vb_ref, zb_ref, upb_ref, bb_ref, ob_ref, sb_ref), False)
    jobs = []
    for s in range(n_pairs):
        jobs.append((fwd, slice(s * GLA_PAIR, (s + 1) * GLA_PAIR)))
        jobs.append((bwd, slice((n_pairs - 1 - s) * GLA_PAIR, (n_pairs - s) * GLA_PAIR)))

    cums = []
    for d, rows in jobs:
        g_hi, g_lo = _split2(d.g[rows])
        b = _bdot(d.left2_m, jnp.concatenate([g_hi, g_lo], axis=0))
        cums.append((b, b.T))

    ops = []
    for (d, rows), (b, bt) in zip(jobs, cums):
        def rowpick(a, i):
            return jnp.where(d.row_low, a[i:i + 1, :], a[c + i:c + i + 1, :])

        def colpick(a, i):
            return jnp.where(d.col_low, a[:, i:i + 1], a[:, c + i:c + i + 1])

        q = d.q_ref[rows, :].astype(F32)
        kt = d.kt_ref[:, rows].astype(F32)
        qe = (q * jnp.exp(b - rowpick(b, d.ref_i))).astype(BF16)
        q_in = q * jnp.exp(b)
        ket = (kt * jnp.exp(colpick(bt, d.ref_i) - bt)).astype(BF16)
        kst = kt * jnp.exp(colpick(bt, d.last_i) - bt)
        dec_first_row = jnp.exp(b[d.i_first:d.i_first + 1, :])
        dec_second_col = jnp.exp(bt[:, d.i_second:d.i_second + 1])
        dec_total_col = jnp.exp(bt[:, d.i_first:d.i_first + 1] + bt[:, d.i_second:d.i_second + 1])
        q_state = (q_in * jnp.where(d.second_rows, dec_first_row, 1.0)).astype(BF16)
        k_state = (kst * jnp.where(d.first_cols, dec_second_col, 1.0)).astype(BF16)
        ops.append((qe, ket, q_in.astype(BF16), kst.astype(BF16), q_state, k_state, dec_total_col))

    scores = []
    for qe, ket, q_in, kst, _, _, _ in ops:
        scores.append((_bdot(qe, ket), _bdot(q_in, kst)))

    locals_ = []
    for (d, rows), (a_diag, a_cross), op in zip(jobs, scores, ops):
        a = jnp.where(d.causal, a_diag, jnp.where(d.cross, a_cross, 0.0)).astype(BF16)
        v = d.v_ref[rows, :]
        locals_.append((jnp.concatenate([a, op[4]], axis=1), v, _bdot(op[5], v)))

    state = {id(fwd): sf_ref[...], id(bwd): sb_ref[...]}
    for (d, rows), (a_q, v, upd), op in zip(jobs, locals_, ops):
        s = state[id(d)]
        o = _bdot(a_q, jnp.concatenate([v, s.astype(BF16)], axis=0))
        d.o_ref[rows, :] = o.astype(d.o_ref.dtype)
        state[id(d)] = op[6] * s + upd
    sf_ref[...] = state[id(fwd)]
    sb_ref[...] = state[id(bwd)]


def _gla(proj, kt, z, gf_up, gf_b, gb_up, gb_b, batch, seq):
    t_rows = batch * seq
    ts = min(GLA_ROWS, seq)
    n_s = seq // ts
    q_col0 = GLA_Q_COL // GLA_DK
    v_col0 = GLA_V_COL // GLA_DV

    zero = jnp.zeros((GATE_RANK, GLA_KEY_WIDTH), F32)
    upf = jnp.concatenate([gf_up, zero], axis=0)
    upb = jnp.concatenate([zero, gb_up], axis=0)
    bf = gf_b.reshape(1, GLA_KEY_WIDTH)
    bb = gb_b.reshape(1, GLA_KEY_WIDTH)

    def tile_specs(tile):
        return [
            pl.BlockSpec((ts, GLA_DK), lambda b, h, i: (tile(b, i), q_col0 + h)),
            pl.BlockSpec((GLA_DK, ts), lambda b, h, i: (h, tile(b, i))),
            pl.BlockSpec((ts, GLA_DV), lambda b, h, i: (tile(b, i), v_col0 + h)),
            pl.BlockSpec((ts, 2 * GATE_RANK), lambda b, h, i: (tile(b, i), 0)),
        ]

    fwd_tile = lambda b, i: b * n_s + i
    bwd_tile = lambda b, i: b * n_s + (n_s - 1 - i)
    gate_specs = [
        pl.BlockSpec((2 * GATE_RANK, GLA_DK), lambda b, h, i: (0, h)),
        pl.BlockSpec((1, GLA_DK), lambda b, h, i: (0, h)),
    ]
    return pl.pallas_call(
        _gla_kernel,
        name="gla",
        grid=(batch, GLA_HEADS, n_s),
        in_specs=tile_specs(fwd_tile) + tile_specs(bwd_tile) + gate_specs + gate_specs,
        out_specs=[
            pl.BlockSpec((ts, GLA_DV), lambda b, h, i: (fwd_tile(b, i), h)),
            pl.BlockSpec((ts, GLA_DV), lambda b, h, i: (bwd_tile(b, i), h)),
        ],
        out_shape=[jax.ShapeDtypeStruct((t_rows, GLA_VALUE_WIDTH), BF16)] * 2,
        scratch_shapes=[pltpu.VMEM((GLA_DK, GLA_DV), F32), pltpu.VMEM((GLA_DK, GLA_DV), F32)],
        compiler_params=_params(("parallel", "parallel", "arbitrary")),
    )(proj, kt, proj, z, proj, kt, proj, z, upf, bf, upb, bb)


def _outproj_kernel(o1_ref, o2_ref, o3_ref, l1_ref, l2_ref, l3_ref, of_ref, ob_ref, gr_ref, x_ref,
                    ag_ref, gg_ref, w_ref, h_ref, ao_ref, cat_a, cat_b):
    i = pl.program_id(0)
    rows = x_ref.shape[0]

    @pl.when(i == 0)
    def _():
        cat_b[...] = jnp.zeros_like(cat_b)

    def step(cur, prev):
        l1, l2, l3 = l1_ref[...], l2_ref[...], l3_ref[...]
        m = jnp.maximum(jnp.maximum(l1, l2), l3)
        e1, e2, e3 = jnp.exp(l1 - m), jnp.exp(l2 - m), jnp.exp(l3 - m)
        inv = 1.0 / (e1 + e2 + e3)
        w1, w2, w3 = e1 * inv, e2 * inv, e3 * inv

        ssq = jnp.zeros((rows, 1), F32)
        for h in range(ATTN_HEADS):
            hs = slice(h * HEAD_DIM, (h + 1) * HEAD_DIM)
            lc = (h // ATTN_HEAD_GROUP) * LANES + h % ATTN_HEAD_GROUP
            ao = (w1[:, lc:lc + 1] * o1_ref[:, hs].astype(F32) + w2[:, lc:lc + 1] * o2_ref[:, hs].astype(F32)
                  + w3[:, lc:lc + 1] * o3_ref[:, hs].astype(F32))
            ssq = ssq + jnp.sum(ao * ao, axis=-1, keepdims=True)
            ao_ref[:, hs] = ao
        inv_rms = lax.rsqrt(ssq * (1.0 / ATTN_WIDTH) + EPS)
        cur[:, 0:ATTN_WIDTH] = (ao_ref[...] * inv_rms * ag_ref[...]).astype(BF16)

        for h in range(GLA_HEADS):
            hs = slice(h * GLA_DV, (h + 1) * GLA_DV)
            o = of_ref[:, hs].astype(F32) + ob_ref[:, hs].astype(F32)
            y = o * lax.rsqrt(jnp.mean(o * o, axis=-1, keepdims=True) + EPS) * gg_ref[...]
            gate = gr_ref[:, hs].astype(F32)
            y = y * (gate / (1.0 + jnp.exp(-gate)))
            cur[:, ATTN_WIDTH + h * GLA_DV:ATTN_WIDTH + (h + 1) * GLA_DV] = y.astype(BF16)

        h_ref[...] = x_ref[...] + _bdot(prev[...], w_ref[...])

    @pl.when(i % 2 == 0)
    def _():
        step(cat_a, cat_b)

    @pl.when(i % 2 == 1)
    def _():
        step(cat_b, cat_a)


def _outproj(branches, o_f, o_b, proj, x2, attn_g, gla_g, w_out):
    t_rows = x2.shape[0]
    tm = OUTPROJ_ROWS
    n_tiles = t_rows // tm
    mixed = lambda i: jnp.minimum(i, n_tiles - 1)
    projected = lambda i: jnp.maximum(i - 1, 0)
    mix = lambda width: pl.BlockSpec((tm, width), lambda i: (mixed(i), 0))
    (o1, l1), (o2, l2), (o3, l3) = branches
    lse_w = ATTN_GROUPS * LANES
    cat = pltpu.VMEM((tm, D_MODEL), BF16)
    return pl.pallas_call(
        _outproj_kernel,
        name="outproj",
        grid=(n_tiles + 1,),
        in_specs=[
            mix(ATTN_WIDTH), mix(ATTN_WIDTH), mix(ATTN_WIDTH), mix(lse_w), mix(lse_w), mix(lse_w),
            mix(GLA_VALUE_WIDTH), mix(GLA_VALUE_WIDTH),
            pl.BlockSpec((tm, GLA_VALUE_WIDTH), lambda i: (mixed(i), GLA_GATE_COL // GLA_VALUE_WIDTH)),
            pl.BlockSpec((tm, D_MODEL), lambda i: (projected(i), 0)),
            pl.BlockSpec((1, ATTN_WIDTH), lambda i: (0, 0)),
            pl.BlockSpec((1, GLA_DV), lambda i: (0, 0)),
            pl.BlockSpec((D_MODEL, D_MODEL), lambda i: (0, 0), pipeline_mode=pl.Buffered(1)),
        ],
        out_specs=pl.BlockSpec((tm, D_MODEL), lambda i: (projected(i), 0)),
        out_shape=jax.ShapeDtypeStruct((t_rows, D_MODEL), F32),
        scratch_shapes=[pltpu.VMEM((tm, ATTN_WIDTH), F32), cat, cat],
        compiler_params=_params(("arbitrary",)),
    )(o1, o2, o3, l1, l2, l3, o_f, o_b, proj, x2, attn_g, gla_g, w_out)


def _ffn_kernel(h_ref, hp_ref, hn_ref, g2_ref, wg_ref, wu_ref, cw_ref, cb_ref, wd_ref, gf_ref,
                out_ref, n_ref, *, tiles_per_seq, final_norm):
    i = pl.program_id(0)
    j = pl.program_id(1)
    tm = h_ref.shape[0]
    halo = FFN_HALO

    def norm(x):
        return (x * lax.rsqrt(jnp.mean(x * x, axis=-1, keepdims=True) + EPS) * g2_ref[...]).astype(BF16)

    @pl.when(j == 0)
    def _():
        n_ref[0:halo] = norm(hp_ref[...])
        n_ref[halo:halo + tm] = norm(h_ref[...])
        n_ref[halo + tm:] = norm(hn_ref[...])
        out_ref[...] = h_ref[...]

    gate = _bdot(n_ref[...], wg_ref[...])
    ext = tm + 2 * halo
    row = lax.broadcasted_iota(jnp.int32, (tm, 1), 0)
    seq_tile = i % tiles_per_seq
    g_prev = jnp.where((row == 0) & (seq_tile == 0), 0.0, pltpu.roll(gate, 1, 0)[halo:halo + tm, :])
    g_next = jnp.where((row == tm - 1) & (seq_tile == tiles_per_seq - 1), 0.0,
                       pltpu.roll(gate, ext - 1, 0)[halo:halo + tm, :])
    g_mid = gate[halo:halo + tm, :]
    cw = cw_ref[...]
    conv = g_prev * cw[0:1, :] + g_mid * cw[1:2, :] + g_next * cw[2:3, :] + cb_ref[...]
    up = _bdot(n_ref[halo:halo + tm, :], wu_ref[...])
    act = (conv / (1.0 + jnp.exp(-conv))) * up
    out_ref[...] += _bdot(act.astype(BF16), wd_ref[...])

    if final_norm:
        @pl.when(j == pl.num_programs(1) - 1)
        def _():
            y = out_ref[...]
            out_ref[...] = y * lax.rsqrt(jnp.mean(y * y, axis=-1, keepdims=True) + EPS) * gf_ref[...]


def _ffn(h2, g2, wg, wu, conv_w, conv_b, wd, gfinal, seq, final_norm):
    t_rows = h2.shape[0]
    tm = min(FFN_ROWS, seq)
    tf = FFN_COLS
    halo_blocks_per_tile = tm // FFN_HALO
    n_halo_blocks = t_rows // FFN_HALO
    return pl.pallas_call(
        functools.partial(_ffn_kernel, tiles_per_seq=seq // tm, final_norm=final_norm),
        name="ffn",
        grid=(t_rows // tm, D_FF // tf),
        in_specs=[
            pl.BlockSpec((tm, D_MODEL), lambda i, j: (i, 0)),
            pl.BlockSpec((FFN_HALO, D_MODEL), lambda i, j: (jnp.maximum(i * halo_blocks_per_tile - 1, 0), 0)),
            pl.BlockSpec((FFN_HALO, D_MODEL),
                         lambda i, j: (jnp.minimum((i + 1) * halo_blocks_per_tile, n_halo_blocks - 1), 0)),
            pl.BlockSpec((1, D_MODEL), lambda i, j: (0, 0)),
            pl.BlockSpec((D_MODEL, tf), lambda i, j: (0, j)),
            pl.BlockSpec((D_MODEL, tf), lambda i, j: (0, j)),
            pl.BlockSpec((CONV_WIDTH, tf), lambda i, j: (0, j)),
            pl.BlockSpec((1, tf), lambda i, j: (0, j)),
            pl.BlockSpec((tf, D_MODEL), lambda i, j: (j, 0)),
            pl.BlockSpec((1, D_MODEL), lambda i, j: (0, 0)),
        ],
        out_specs=pl.BlockSpec((tm, D_MODEL), lambda i, j: (i, 0)),
        out_shape=jax.ShapeDtypeStruct((t_rows, D_MODEL), F32),
        scratch_shapes=[pltpu.VMEM((tm + 2 * FFN_HALO, D_MODEL), BF16)],
        compiler_params=_params(("parallel", "arbitrary")),
    )(h2, h2, h2, g2, wg, wu, conv_w, conv_b, wd, gfinal)


def _rope_tables(seq, identity_rows):
    pos = np.arange(seq, dtype=np.float32)
    inv_freq = np.float32(ROPE_THETA) ** (-np.arange(0, ROPE_DIM, 2, dtype=np.float32) / np.float32(ROPE_DIM))
    ang = (pos[:, None] * inv_freq.astype(np.float32)[None, :]).astype(np.float32)
    cos = np.cos(ang.astype(np.float64)).astype(np.float32)
    sin = np.sin(ang.astype(np.float64)).astype(np.float32)
    gap = ROPE_PARTNER_SHIFT - ROPE_HALF
    ones, zeros = np.ones((seq, gap), np.float32), np.zeros((seq, gap), np.float32)
    cos_t = np.concatenate([cos, ones, cos, ones], axis=1)
    sin_t = np.concatenate([-sin, zeros, sin, zeros], axis=1)
    cos_t = np.concatenate([cos_t, np.ones((identity_rows, HEAD_DIM), np.float32)], axis=0)
    sin_t = np.concatenate([sin_t, np.zeros((identity_rows, HEAD_DIM), np.float32)], axis=0)
    return jnp.asarray(cos_t), jnp.asarray(sin_t)


def _rotary_head_layout(w):
    d_in, width = w.shape
    w = w.reshape(d_in, width // HEAD_DIM, HEAD_DIM)
    split = ROPE_DIM + ROPE_PARTNER_SHIFT - ROPE_HALF
    w = jnp.concatenate([w[..., :ROPE_HALF], w[..., ROPE_DIM:split], w[..., ROPE_HALF:ROPE_DIM], w[..., split:]],
                        axis=-1)
    return w.reshape(d_in, width)


def kernel(x, norm1_g, w_in, gf_up, gf_b, gb_up, gb_b, gla_norm_g, attn_norm_g, w_out, norm2_g, w_gate, w_up,
           conv_w, conv_b, w_down, final_norm_g):
    batch, seq, d_model = x.shape
    depth = norm1_g.shape[0]
    aw, kw, vw = ATTN_WIDTH, GLA_KEY_WIDTH, GLA_VALUE_WIDTH
    assert d_model == D_MODEL and w_in.shape[-1] == 3 * aw + 2 * kw + 2 * vw + 2 * GATE_RANK
    assert all((window // 2) // dil == N_SIDE for window, dil in DILATED_PATTERNS) and DILATIONS == (1, 4, 16)
    assert seq % ATTN_TILE == 0

    cos_t, sin_t = _rope_tables(seq, min(INPROJ_ROWS, seq))
    c_gq, c_gk, c_gv, c_gr, c_z = 3 * aw, 3 * aw + kw, 3 * aw + 2 * kw, 3 * aw + 2 * kw + vw, 3 * aw + 2 * kw + 2 * vw

    h2 = x.reshape(batch * seq, D_MODEL)
    for l in range(depth):
        wi = w_in[l]
        w_main = jnp.concatenate([_rotary_head_layout(wi[:, :2 * aw]), wi[:, 2 * aw:c_gq], wi[:, c_gv:c_gr],
                                  wi[:, c_gr:c_z], wi[:, c_gq:c_gk]], axis=1).astype(BF16)
        w_kz = jnp.concatenate([wi[:, c_gk:c_gv], wi[:, c_z:]], axis=1).T.astype(BF16)

        proj, proj4, proj16, kt, z = _inproj(
            h2, norm1_g[l].reshape(1, D_MODEL), w_main, w_kz, cos_t, sin_t, batch, seq)
        by_residue = {1: proj.reshape(batch, 1, seq, PROJ_WIDTH), 4: proj4, 16: proj16}
        branches = [_attention_branch(by_residue[dil], batch, seq, dil) for dil in DILATIONS]
        o_f, o_b = _gla(proj, kt, z, gf_up[l], gf_b[l], gb_up[l], gb_b[l], batch, seq)
        h2 = _outproj(branches, o_f, o_b, proj, h2, attn_norm_g[l].reshape(1, ATTN_WIDTH),
                      gla_norm_g[l].reshape(1, GLA_DV), w_out[l].astype(BF16))
        h2 = _ffn(h2, norm2_g[l].reshape(1, D_MODEL), w_gate[l].astype(BF16), w_up[l].astype(BF16), conv_w[l],
                  conv_b[l].reshape(1, D_FF), w_down[l].astype(BF16), final_norm_g.reshape(1, D_MODEL), seq,
                  final_norm=(l == depth - 1))
    return h2.reshape(batch, seq, D_MODEL)
```

```python
import functools

import jax
import jax.numpy as jnp
import numpy as np
from jax import lax
from jax.experimental import pallas as pl
from jax.experimental.pallas import tpu as pltpu

F32 = jnp.float32
BF16 = jnp.bfloat16

D_MODEL = 2048
ATTN_WIDTH = 1024
HEAD_DIM = 128
ATTN_HEADS = ATTN_WIDTH // HEAD_DIM
DILATED_PATTERNS = ((128, 1), (512, 4), (2048, 16))
DILATIONS = tuple(d for _, d in DILATED_PATTERNS)
N_SIDE = 64
ROPE_THETA = 500000.0
ROPE_DIM = HEAD_DIM // 4
ROPE_HALF = ROPE_DIM // 2
GLA_HEADS = 4
GLA_DK = 128
GLA_DV = 256
GLA_KEY_WIDTH = GLA_HEADS * GLA_DK
GLA_VALUE_WIDTH = GLA_HEADS * GLA_DV
GATE_RANK = 16
GATE_NORMALIZER = 16.0
GLA_CHUNK = 64
D_FF = 5632
CONV_WIDTH = 3
EPS = 1e-6

LANES = 128
BF16_SUBLANES = 16
VMEM_LIMIT_BYTES = 60 * 1024 * 1024

INPROJ_ROWS = 1024
INPROJ_COLS = 512
INPROJ_NORM_CHUNKS = 4
ATTN_TILE = 4096
ATTN_QBLK = 128
ATTN_HEAD_GROUP = 4
ATTN_GROUP_WIDTH = ATTN_HEAD_GROUP * HEAD_DIM
ATTN_GROUPS = ATTN_HEADS // ATTN_HEAD_GROUP
GLA_ROWS = 2048
GLA_PAIR = 2 * GLA_CHUNK
OUTPROJ_ROWS = 512
FFN_ROWS = 1024
FFN_COLS = 512
FFN_HALO = BF16_SUBLANES

ATTN_PROJ_WIDTH = 3 * ATTN_WIDTH
PROJ_WIDTH = ATTN_PROJ_WIDTH + 2 * GLA_VALUE_WIDTH + GLA_KEY_WIDTH
N_COL_TILES = PROJ_WIDTH // INPROJ_COLS
N_ATTN_COL_TILES = ATTN_PROJ_WIDTH // INPROJ_COLS
RESIDUE_WIDTH = ATTN_PROJ_WIDTH
GLA_V_COL = ATTN_PROJ_WIDTH
GLA_GATE_COL = GLA_V_COL + GLA_VALUE_WIDTH
GLA_Q_COL = GLA_GATE_COL + GLA_VALUE_WIDTH
ROPE_PARTNER_SHIFT = HEAD_DIM // 2

_NT = (((1,), (1,)), ((), ()))


def _params(semantics):
    return pltpu.CompilerParams(dimension_semantics=semantics, vmem_limit_bytes=VMEM_LIMIT_BYTES)


def _bdot(a, b):
    return jnp.dot(a, b, preferred_element_type=F32)


def _inproj_kernel(x_ref, g_ref, w_ref, wkz_ref, cos_ref, sin_ref,
                   p_ref, p4_ref, p16_ref, kt_ref, z_ref, n_ref, rs_a, rs_b, g4_ref):
    i = pl.program_id(0)
    j = pl.program_id(1)
    rows = x_ref.shape[0]
    heads_per_tile = INPROJ_COLS // HEAD_DIM
    live = i < pl.num_programs(0) - 1

    @pl.when((i == 0) & (j == 0))
    def _():
        rs_b[...] = jnp.zeros_like(rs_b)

    @pl.when((j == 0) & live)
    def _():
        chunk = rows // INPROJ_NORM_CHUNKS
        for c in range(INPROJ_NORM_CHUNKS):
            rs = slice(c * chunk, (c + 1) * chunk)
            x = x_ref[rs, :]
            inv = lax.rsqrt(jnp.mean(x * x, axis=-1, keepdims=True) + EPS)
            n = (x * inv * g_ref[...]).astype(BF16)
            n_ref[rs, :] = n
            kz = lax.dot_general(wkz_ref[...], n, _NT, preferred_element_type=F32)
            kt_ref[:, rs] = kz[0:GLA_KEY_WIDTH, :].astype(BF16)
            z_ref[rs, :] = kz[GLA_KEY_WIDTH:, :].T

    done = jnp.where(j == 0, N_COL_TILES - 1, j - 1)
    scaled = (done < ATTN_WIDTH // INPROJ_COLS) | (done == N_COL_TILES - 1)
    scale = jnp.where(scaled, HEAD_DIM ** -0.5, 1.0).astype(F32)

    def step(cur, prev, regroup):
        cos = cos_ref[...]
        sin = sin_ref[...]
        for h in range(heads_per_tile):
            sl = slice(h * HEAD_DIM, (h + 1) * HEAD_DIM)
            t = prev[h]
            r = (t * cos + pltpu.roll(t, ROPE_PARTNER_SHIFT, 1) * sin) * scale
            p_ref[:, sl] = r.astype(BF16)
            if regroup:
                prev[h] = r
        if regroup:
            per4 = rows // 4
            per16 = rows // 16
            for h in range(heads_per_tile):
                sl = slice(h * HEAD_DIM, (h + 1) * HEAD_DIM)
                for res4 in range(4):
                    grp = prev[h, pl.ds(res4, per4, stride=4), :]
                    p4_ref[0, res4, :, sl] = grp.astype(BF16)
                    g4_ref[h, res4 * per4:(res4 + 1) * per4, :] = grp
                for res4 in range(4):
                    for a in range(4):
                        p16_ref[0, 4 * a + res4, :, sl] = (
                            g4_ref[h, pl.ds(res4 * per4 + a, per16, stride=4), :].astype(BF16))
        acc = _bdot(n_ref[...], w_ref[...])
        for h in range(heads_per_tile):
            cur[h] = acc[:, h * HEAD_DIM:(h + 1) * HEAD_DIM]

    attn_tile = (j >= 1) & (j <= N_ATTN_COL_TILES)
    parity = (i * N_COL_TILES + j) % 2
    work = live | (j == 0)
    for pv, (cur, prev) in enumerate(((rs_a, rs_b), (rs_b, rs_a))):
        for regroup in (True, False):
            @pl.when((parity == pv) & work & (attn_tile if regroup else jnp.logical_not(attn_tile)))
            def _(cur=cur, prev=prev, regroup=regroup):
                step(cur, prev, regroup)


def _inproj(x2, g1, w_main, w_kz, cos_t, sin_t, batch, seq):
    t_rows = x2.shape[0]
    tm = min(INPROJ_ROWS, seq)
    tps = seq // tm
    n_rows = t_rows // tm
    last_col = N_COL_TILES - 1
    row = lambda i: jnp.minimum(i, n_rows - 1)
    n_rotary_tiles = 2 * ATTN_WIDTH // INPROJ_COLS
    table_spec = pl.BlockSpec(
        (tm, HEAD_DIM), lambda i, j: (jnp.where((j >= 1) & (j <= n_rotary_tiles), row(i) % tps, tps), 0))

    def finished(i, j):
        r = jnp.where(j == 0, jnp.maximum(i - 1, 0), row(i))
        c = jnp.where(j == 0, jnp.where(i == 0, 0, last_col), jnp.where(i == n_rows, last_col, j - 1))
        return r, c

    def residue_spec(dil):
        def index(i, j):
            c = jnp.where(i == n_rows, N_ATTN_COL_TILES - 1, jnp.clip(j - 1, 0, N_ATTN_COL_TILES - 1))
            return row(i) // tps, 0, row(i) % tps, c
        return pl.BlockSpec((1, dil, tm // dil, INPROJ_COLS), index)

    slab = pltpu.VMEM((INPROJ_COLS // HEAD_DIM, tm, HEAD_DIM), F32)
    return pl.pallas_call(
        _inproj_kernel,
        name="inproj",
        grid=(n_rows + 1, N_COL_TILES),
        in_specs=[
            pl.BlockSpec((tm, D_MODEL), lambda i, j: (row(i), 0)),
            pl.BlockSpec((1, D_MODEL), lambda i, j: (0, 0)),
            pl.BlockSpec((D_MODEL, INPROJ_COLS), lambda i, j: (0, j)),
            pl.BlockSpec((GLA_KEY_WIDTH + 2 * GATE_RANK, D_MODEL), lambda i, j: (0, 0)),
            table_spec,
            table_spec,
        ],
        out_specs=[
            pl.BlockSpec((tm, INPROJ_COLS), finished),
            residue_spec(4),
            residue_spec(16),
            pl.BlockSpec((GLA_KEY_WIDTH, tm), lambda i, j: (0, row(i))),
            pl.BlockSpec((tm, 2 * GATE_RANK), lambda i, j: (row(i), 0)),
        ],
        out_shape=[
            jax.ShapeDtypeStruct((t_rows, PROJ_WIDTH), BF16),
            jax.ShapeDtypeStruct((batch, 4, seq // 4, RESIDUE_WIDTH), BF16),
            jax.ShapeDtypeStruct((batch, 16, seq // 16, RESIDUE_WIDTH), BF16),
            jax.ShapeDtypeStruct((GLA_KEY_WIDTH, t_rows), BF16),
            jax.ShapeDtypeStruct((t_rows, 2 * GATE_RANK), F32),
        ],
        scratch_shapes=[pltpu.VMEM((tm, D_MODEL), BF16), slab, slab, slab],
        compiler_params=_params(("arbitrary", "arbitrary")),
    )(x2, g1, w_main, w_kz, cos_t, sin_t)


def _attn_kernel(q_ref, kp_ref, kc_ref, kn_ref, vp_ref, vc_ref, vn_ref, o_ref, lse_ref, osc, *, length, dil):
    t = pl.program_id(2)
    tq = q_ref.shape[2]
    win = ATTN_QBLK + 2 * N_SIDE
    n_blk = tq // ATTN_QBLK

    def window(prev_ref, cur_ref, next_ref, res, blk, cols):
        lo, hi = blk * ATTN_QBLK - N_SIDE, blk * ATTN_QBLK + ATTN_QBLK + N_SIDE
        parts = []
        if lo < 0:
            parts.append(prev_ref[0, res, :, cols])
        parts.append(cur_ref[0, res, max(lo, 0):min(hi, tq), cols])
        if hi > tq:
            parts.append(next_ref[0, res, :, cols])
        return parts[0] if len(parts) == 1 else jnp.concatenate(parts, axis=0)

    ones = jnp.ones((win, HEAD_DIM), BF16)

    row = lax.broadcasted_iota(jnp.int32, (ATTN_QBLK, win), 0)
    col = lax.broadcasted_iota(jnp.int32, (ATTN_QBLK, win), 1)
    band = (col >= row) & (col <= row + 2 * N_SIDE)
    lane = lax.broadcasted_iota(jnp.int32, (ATTN_QBLK, LANES), 1)

    for res, blk in [(a, b) for a in range(dil) for b in range(n_blk)]:
        r0 = blk * ATTN_QBLK
        first_key = t * tq + r0 - N_SIDE
        if 0 < blk < n_blk - 1:
            valid = band
        else:
            valid = band & (col >= -first_key) & (col < length - first_key)
        if dil == 1:
            dst = pl.ds(r0, ATTN_QBLK)
        else:
            dst = pl.ds(r0 * dil + res, ATTN_QBLK, stride=dil)
        lse_tile = jnp.zeros((ATTN_QBLK, LANES), F32)
        for h in range(ATTN_HEAD_GROUP):
            hs = slice(h * HEAD_DIM, (h + 1) * HEAD_DIM)
            q = q_ref[0, res, pl.ds(r0, ATTN_QBLK), hs]
            k = window(kp_ref, kc_ref, kn_ref, res, blk, hs)
            v_ones = jnp.concatenate([window(vp_ref, vc_ref, vn_ref, res, blk, hs), ones], axis=1)
            s = lax.dot_general(q, k, _NT, preferred_element_type=F32)
            s = jnp.where(valid, s, -1e30)
            m = jnp.max(s, axis=-1, keepdims=True)
            p = jnp.exp(s - m)
            o_den = _bdot(p.astype(BF16), v_ones)
            den = o_den[:, HEAD_DIM:]
            osc[h, dst, :] = o_den[:, :HEAD_DIM] / den
            lse_tile = jnp.where(lane == h, m + jnp.log(den), lse_tile)
        lse_ref[dst, :] = lse_tile

    for h in range(ATTN_HEAD_GROUP):
        o_ref[:, h * HEAD_DIM:(h + 1) * HEAD_DIM] = osc[h].astype(o_ref.dtype)


def _attention_branch(proj_res, batch, seq, dil):
    length = seq // dil
    tn = min(ATTN_TILE, seq)
    tq = tn // dil
    n_tiles = seq // tn
    halo_per_tile = tq // N_SIDE
    n_halo_blocks = length // N_SIDE
    gw = ATTN_GROUP_WIDTH
    part_blocks = ATTN_WIDTH // gw

    def cur(part):
        return pl.BlockSpec((1, dil, tq, gw), lambda b, g, t: (b, 0, t, part * part_blocks + g))

    def prev(part):
        return pl.BlockSpec((1, dil, N_SIDE, gw),
                            lambda b, g, t: (b, 0, jnp.maximum(t * halo_per_tile - 1, 0), part * part_blocks + g))

    def nxt(part):
        return pl.BlockSpec((1, dil, N_SIDE, gw),
                            lambda b, g, t: (b, 0, jnp.minimum((t + 1) * halo_per_tile, n_halo_blocks - 1),
                                             part * part_blocks + g))

    return pl.pallas_call(
        functools.partial(_attn_kernel, length=length, dil=dil),
        name=f"attn_d{dil}",
        grid=(batch, ATTN_GROUPS, n_tiles),
        in_specs=[cur(0), prev(1), cur(1), nxt(1), prev(2), cur(2), nxt(2)],
        out_specs=[
            pl.BlockSpec((tn, gw), lambda b, g, t: (b * n_tiles + t, g)),
            pl.BlockSpec((tn, LANES), lambda b, g, t: (b * n_tiles + t, g)),
        ],
        out_shape=[
            jax.ShapeDtypeStruct((batch * seq, ATTN_WIDTH), BF16),
            jax.ShapeDtypeStruct((batch * seq, ATTN_GROUPS * LANES), F32),
        ],
        scratch_shapes=[pltpu.VMEM((ATTN_HEAD_GROUP, tn, HEAD_DIM), F32)],
        compiler_params=_params(("parallel", "parallel", "parallel")),
    )(proj_res, proj_res, proj_res, proj_res, proj_res, proj_res, proj_res)


def _split2(a):
    hi = a.astype(BF16)
    lo = (a - hi.astype(F32)).astype(BF16)
    return hi, lo


def _dot_f32(a, b):
    a_hi, a_lo = _split2(a)
    b_hi, b_lo = _split2(b)
    return _bdot(a_hi, b_hi) + (_bdot(a_hi, b_lo) + _bdot(a_lo, b_hi))


def _log_sigmoid(x):
    return jnp.minimum(x, 0.0) - jnp.log(1.0 + jnp.exp(-jnp.abs(x)))


class _Direction:
    def __init__(self, refs, forward):
        self.q_ref, self.kt_ref, self.v_ref, z_ref, up_ref, bias_ref, self.o_ref, self.state_ref = refs
        c = GLA_CHUNK
        r = lax.broadcasted_iota(jnp.int32, (GLA_PAIR, GLA_PAIR), 0)
        cc = lax.broadcasted_iota(jnp.int32, (GLA_PAIR, GLA_PAIR), 1)
        same = jnp.where(r < c, 0, 1) == jnp.where(cc < c, 0, 1)
        lower = same & (r >= cc)
        upper = same & (r <= cc)
        self.row_low = r < c
        self.col_low = cc < c
        if forward:
            left, self.causal = lower, lower
            self.ref_i, self.last_i = c // 2 - 1, c - 1
            self.cross = (r >= c) & (cc < c)
            self.i_first, self.i_second = self.last_i, c + self.last_i
            self.second_rows, self.first_cols = r >= c, cc < c
        else:
            left, self.causal = upper, upper
            self.ref_i, self.last_i = c // 2, 0
            self.cross = (r < c) & (cc >= c)
            self.i_first, self.i_second = c + self.last_i, self.last_i
            self.second_rows, self.first_cols = r < c, cc >= c
        left_m = jnp.where(left, 1.0, 0.0).astype(BF16)
        self.left2_m = jnp.concatenate([left_m, left_m], axis=1)
        eye = jnp.where(r == cc, 1.0, 0.0).astype(BF16)
        self.eye2 = jnp.concatenate([eye, eye], axis=1)
        inv_norm = 1.0 / GATE_NORMALIZER
        self.g = _log_sigmoid(_dot_f32(z_ref[...], up_ref[...]) + bias_ref[...]) * inv_norm


def _gla_kernel(qf_ref, ktf_ref, vf_ref, zf_ref, qb_ref, ktb_ref, vb_ref, zb_ref,
                upf_ref, bf_ref, upb_ref, bb_ref, of_ref, ob_ref, sf_ref, sb_ref):
    @pl.when(pl.program_id(2) == 0)
    def _():
        sf_ref[...] = jnp.zeros_like(sf_ref)
        sb_ref[...] = jnp.zeros_like(sb_ref)

    c = GLA_CHUNK
    ts = qf_ref.shape[0]
    n_pairs = ts // GLA_PAIR
    fwd = _Direction((qf_ref, ktf_ref, vf_ref, zf_ref, upf_ref, bf_ref, of_ref, sf_ref), True)
    bwd = _Direction((qb_ref, ktb_ref, vb_ref, zb_ref, upb_ref, bb_ref, ob_ref, sb_ref), False)
    jobs = []
    for s in range(n_pairs):
        jobs.append((fwd, slice(s * GLA_PAIR, (s + 1) * GLA_PAIR)))
        jobs.append((bwd, slice((n_pairs - 1 - s) * GLA_PAIR, (n_pairs - s) * GLA_PAIR)))

    cums = []
    for d, rows in jobs:
        g_hi, g_lo = _split2(d.g[rows])
        b = _bdot(d.left2_m, jnp.concatenate([g_hi, g_lo], axis=0))
        b_hi, b_lo = _split2(b)
        bt = lax.dot_general(d.eye2, jnp.concatenate([b_hi, b_lo], axis=1), _NT, preferred_element_type=F32)
        cums.append((b, bt))

    ops = []
    for (d, rows), (b, bt) in zip(jobs, cums):
        def rowpick(a, i):
            return jnp.where(d.row_low, a[i:i + 1, :], a[c + i:c + i + 1, :])

        def colpick(a, i):
            return jnp.where(d.col_low, a[:, i:i + 1], a[:, c + i:c + i + 1])

        q = d.q_ref[rows, :].astype(F32)
        kt = d.kt_ref[:, rows].astype(F32)
        qe = (q * jnp.exp(b - rowpick(b, d.ref_i))).astype(BF16)
        q_in = q * jnp.exp(b)
        ket = (kt * jnp.exp(colpick(bt, d.ref_i) - bt)).astype(BF16)
        kst = kt * jnp.exp(colpick(bt, d.last_i) - bt)
        dec_first_row = jnp.exp(b[d.i_first:d.i_first + 1, :])
        dec_second_col = jnp.exp(bt[:, d.i_second:d.i_second + 1])
        dec_total_col = jnp.exp(bt[:, d.i_first:d.i_first + 1] + bt[:, d.i_second:d.i_second + 1])
        q_state = (q_in * jnp.where(d.second_rows, dec_first_row, 1.0)).astype(BF16)
        k_state = (kst * jnp.where(d.first_cols, dec_second_col, 1.0)).astype(BF16)
        ops.append((qe, ket, q_in.astype(BF16), kst.astype(BF16), q_state, k_state, dec_total_col))

    scores = []
    for qe, ket, q_in, kst, _, _, _ in ops:
        scores.append((_bdot(qe, ket), _bdot(q_in, kst)))

    locals_ = []
    for (d, rows), (a_diag, a_cross), op in zip(jobs, scores, ops):
        a = jnp.where(d.causal, a_diag, jnp.where(d.cross, a_cross, 0.0)).astype(BF16)
        v = d.v_ref[rows, :]
        locals_.append((jnp.concatenate([a, op[4]], axis=1), v, _bdot(op[5], v)))

    state = {id(fwd): sf_ref[...], id(bwd): sb_ref[...]}
    for (d, rows), (a_q, v, upd), op in zip(jobs, locals_, ops):
        s = state[id(d)]
        o = _bdot(a_q, jnp.concatenate([v, s.astype(BF16)], axis=0))
        d.o_ref[rows, :] = o.astype(d.o_ref.dtype)
        state[id(d)] = op[6] * s + upd
    sf_ref[...] = state[id(fwd)]
    sb_ref[...] = state[id(bwd)]


def _gla(proj, kt, z, gf_up, gf_b, gb_up, gb_b, batch, seq):
    t_rows = batch * seq
    ts = min(GLA_ROWS, seq)
    n_s = seq // ts
    q_col0 = GLA_Q_COL // GLA_DK
    v_col0 = GLA_V_COL // GLA_DV

    zero = jnp.zeros((GATE_RANK, GLA_KEY_WIDTH), F32)
    upf = jnp.concatenate([gf_up, zero], axis=0)
    upb = jnp.concatenate([zero, gb_up], axis=0)
    bf = gf_b.reshape(1, GLA_KEY_WIDTH)
    bb = gb_b.reshape(1, GLA_KEY_WIDTH)

    def tile_specs(tile):
        return [
            pl.BlockSpec((ts, GLA_DK), lambda b, h, i: (tile(b, i), q_col0 + h)),
            pl.BlockSpec((GLA_DK, ts), lambda b, h, i: (h, tile(b, i))),
            pl.BlockSpec((ts, GLA_DV), lambda b, h, i: (tile(b, i), v_col0 + h)),
            pl.BlockSpec((ts, 2 * GATE_RANK), lambda b, h, i: (tile(b, i), 0)),
        ]

    fwd_tile = lambda b, i: b * n_s + i
    bwd_tile = lambda b, i: b * n_s + (n_s - 1 - i)
    gate_specs = [
        pl.BlockSpec((2 * GATE_RANK, GLA_DK), lambda b, h, i: (0, h)),
        pl.BlockSpec((1, GLA_DK), lambda b, h, i: (0, h)),
    ]
    return pl.pallas_call(
        _gla_kernel,
        name="gla",
        grid=(batch, GLA_HEADS, n_s),
        in_specs=tile_specs(fwd_tile) + tile_specs(bwd_tile) + gate_specs + gate_specs,
        out_specs=[
            pl.BlockSpec((ts, GLA_DV), lambda b, h, i: (fwd_tile(b, i), h)),
            pl.BlockSpec((ts, GLA_DV), lambda b, h, i: (bwd_tile(b, i), h)),
        ],
        out_shape=[jax.ShapeDtypeStruct((t_rows, GLA_VALUE_WIDTH), BF16)] * 2,
        scratch_shapes=[pltpu.VMEM((GLA_DK, GLA_DV), F32), pltpu.VMEM((GLA_DK, GLA_DV), F32)],
        compiler_params=_params(("parallel", "parallel", "arbitrary")),
    )(proj, kt, proj, z, proj, kt, proj, z, upf, bf, upb, bb)


def _outproj_kernel(o1_ref, o2_ref, o3_ref, l1_ref, l2_ref, l3_ref, of_ref, ob_ref, gr_ref, x_ref,
                    ag_ref, gg_ref, w_ref, h_ref, ao_ref, cat_a, cat_b):
    i = pl.program_id(0)
    rows = x_ref.shape[0]

    @pl.when(i == 0)
    def _():
        cat_b[...] = jnp.zeros_like(cat_b)

    def step(cur, prev):
        l1, l2, l3 = l1_ref[...], l2_ref[...], l3_ref[...]
        m = jnp.maximum(jnp.maximum(l1, l2), l3)
        e1, e2, e3 = jnp.exp(l1 - m), jnp.exp(l2 - m), jnp.exp(l3 - m)
        inv = 1.0 / (e1 + e2 + e3)
        w1, w2, w3 = e1 * inv, e2 * inv, e3 * inv

        ssq = jnp.zeros((rows, 1), F32)
        for h in range(ATTN_HEADS):
            hs = slice(h * HEAD_DIM, (h + 1) * HEAD_DIM)
            lc = (h // ATTN_HEAD_GROUP) * LANES + h % ATTN_HEAD_GROUP
            ao = (w1[:, lc:lc + 1] * o1_ref[:, hs].astype(F32) + w2[:, lc:lc + 1] * o2_ref[:, hs].astype(F32)
                  + w3[:, lc:lc + 1] * o3_ref[:, hs].astype(F32))
            ssq = ssq + jnp.sum(ao * ao, axis=-1, keepdims=True)
            ao_ref[:, hs] = ao
        inv_rms = lax.rsqrt(ssq * (1.0 / ATTN_WIDTH) + EPS)
        cur[:, 0:ATTN_WIDTH] = (ao_ref[...] * inv_rms * ag_ref[...]).astype(BF16)

        for h in range(GLA_HEADS):
            hs = slice(h * GLA_DV, (h + 1) * GLA_DV)
            o = of_ref[:, hs].astype(F32) + ob_ref[:, hs].astype(F32)
            y = o * lax.rsqrt(jnp.mean(o * o, axis=-1, keepdims=True) + EPS) * gg_ref[...]
            gate = gr_ref[:, hs].astype(F32)
            y = y * (gate / (1.0 + jnp.exp(-gate)))
            cur[:, ATTN_WIDTH + h * GLA_DV:ATTN_WIDTH + (h + 1) * GLA_DV] = y.astype(BF16)

        h_ref[...] = x_ref[...] + _bdot(prev[...], w_ref[...])

    @pl.when(i % 2 == 0)
    def _():
        step(cat_a, cat_b)

    @pl.when(i % 2 == 1)
    def _():
        step(cat_b, cat_a)


def _outproj(branches, o_f, o_b, proj, x2, attn_g, gla_g, w_out):
    t_rows = x2.shape[0]
    tm = OUTPROJ_ROWS
    n_tiles = t_rows // tm
    mixed = lambda i: jnp.minimum(i, n_tiles - 1)
    projected = lambda i: jnp.maximum(i - 1, 0)
    mix = lambda width: pl.BlockSpec((tm, width), lambda i: (mixed(i), 0))
    (o1, l1), (o2, l2), (o3, l3) = branches
    lse_w = ATTN_GROUPS * LANES
    cat = pltpu.VMEM((tm, D_MODEL), BF16)
    return pl.pallas_call(
        _outproj_kernel,
        name="outproj",
        grid=(n_tiles + 1,),
        in_specs=[
            mix(ATTN_WIDTH), mix(ATTN_WIDTH), mix(ATTN_WIDTH), mix(lse_w), mix(lse_w), mix(lse_w),
            mix(GLA_VALUE_WIDTH), mix(GLA_VALUE_WIDTH),
            pl.BlockSpec((tm, GLA_VALUE_WIDTH), lambda i: (mixed(i), GLA_GATE_COL // GLA_VALUE_WIDTH)),
            pl.BlockSpec((tm, D_MODEL), lambda i: (projected(i), 0)),
            pl.BlockSpec((1, ATTN_WIDTH), lambda i: (0, 0)),
            pl.BlockSpec((1, GLA_DV), lambda i: (0, 0)),
            pl.BlockSpec((D_MODEL, D_MODEL), lambda i: (0, 0), pipeline_mode=pl.Buffered(1)),
        ],
        out_specs=pl.BlockSpec((tm, D_MODEL), lambda i: (projected(i), 0)),
        out_shape=jax.ShapeDtypeStruct((t_rows, D_MODEL), F32),
        scratch_shapes=[pltpu.VMEM((tm, ATTN_WIDTH), F32), cat, cat],
        compiler_params=_params(("arbitrary",)),
    )(o1, o2, o3, l1, l2, l3, o_f, o_b, proj, x2, attn_g, gla_g, w_out)


def _ffn_kernel(h_ref, hp_ref, hn_ref, g2_ref, wg_ref, wu_ref, cw_ref, cb_ref, wd_ref, gf_ref,
                out_ref, n_ref, *, tiles_per_seq, final_norm):
    i = pl.program_id(0)
    j = pl.program_id(1)
    tm = h_ref.shape[0]
    halo = FFN_HALO

    def norm(x):
        return (x * lax.rsqrt(jnp.mean(x * x, axis=-1, keepdims=True) + EPS) * g2_ref[...]).astype(BF16)

    @pl.when(j == 0)
    def _():
        n_ref[0:halo] = norm(hp_ref[...])
        n_ref[halo:halo + tm] = norm(h_ref[...])
        n_ref[halo + tm:] = norm(hn_ref[...])
        out_ref[...] = h_ref[...]

    gate = _bdot(n_ref[...], wg_ref[...])
    ext = tm + 2 * halo
    row = lax.broadcasted_iota(jnp.int32, (tm, 1), 0)
    seq_tile = i % tiles_per_seq
    g_prev = jnp.where((row == 0) & (seq_tile == 0), 0.0, pltpu.roll(gate, 1, 0)[halo:halo + tm, :])
    g_next = jnp.where((row == tm - 1) & (seq_tile == tiles_per_seq - 1), 0.0,
                       pltpu.roll(gate, ext - 1, 0)[halo:halo + tm, :])
    g_mid = gate[halo:halo + tm, :]
    cw = cw_ref[...]
    conv = g_prev * cw[0:1, :] + g_mid * cw[1:2, :] + g_next * cw[2:3, :] + cb_ref[...]
    up = _bdot(n_ref[halo:halo + tm, :], wu_ref[...])
    act = (conv / (1.0 + jnp.exp(-conv))) * up
    out_ref[...] += _bdot(act.astype(BF16), wd_ref[...])

    if final_norm:
        @pl.when(j == pl.num_programs(1) - 1)
        def _():
            y = out_ref[...]
            out_ref[...] = y * lax.rsqrt(jnp.mean(y * y, axis=-1, keepdims=True) + EPS) * gf_ref[...]


def _ffn(h2, g2, wg, wu, conv_w, conv_b, wd, gfinal, seq, final_norm):
    t_rows = h2.shape[0]
    tm = min(FFN_ROWS, seq)
    tf = FFN_COLS
    halo_blocks_per_tile = tm // FFN_HALO
    n_halo_blocks = t_rows // FFN_HALO
    return pl.pallas_call(
        functools.partial(_ffn_kernel, tiles_per_seq=seq // tm, final_norm=final_norm),
        name="ffn",
        grid=(t_rows // tm, D_FF // tf),
        in_specs=[
            pl.BlockSpec((tm, D_MODEL), lambda i, j: (i, 0)),
            pl.BlockSpec((FFN_HALO, D_MODEL), lambda i, j: (jnp.maximum(i * halo_blocks_per_tile - 1, 0), 0)),
            pl.BlockSpec((FFN_HALO, D_MODEL),
                         lambda i, j: (jnp.minimum((i + 1) * halo_blocks_per_tile, n_halo_blocks - 1), 0)),
            pl.BlockSpec((1, D_MODEL), lambda i, j: (0, 0)),
            pl.BlockSpec((D_MODEL, tf), lambda i, j: (0, j)),
            pl.BlockSpec((D_MODEL, tf), lambda i, j: (0, j)),
            pl.BlockSpec((CONV_WIDTH, tf), lambda i, j: (0, j)),
            pl.BlockSpec((1, tf), lambda i, j: (0, j)),
            pl.BlockSpec((tf, D_MODEL), lambda i, j: (j, 0)),
            pl.BlockSpec((1, D_MODEL), lambda i, j: (0, 0)),
        ],
        out_specs=pl.BlockSpec((tm, D_MODEL), lambda i, j: (i, 0)),
        out_shape=jax.ShapeDtypeStruct((t_rows, D_MODEL), F32),
        scratch_shapes=[pltpu.VMEM((tm + 2 * FFN_HALO, D_MODEL), BF16)],
        compiler_params=_params(("parallel", "arbitrary")),
    )(h2, h2, h2, g2, wg, wu, conv_w, conv_b, wd, gfinal)


def _rope_tables(seq, identity_rows):
    pos = np.arange(seq, dtype=np.float32)
    inv_freq = np.float32(ROPE_THETA) ** (-np.arange(0, ROPE_DIM, 2, dtype=np.float32) / np.float32(ROPE_DIM))
    ang = (pos[:, None] * inv_freq.astype(np.float32)[None, :]).astype(np.float32)
    cos = np.cos(ang.astype(np.float64)).astype(np.float32)
    sin = np.sin(ang.astype(np.float64)).astype(np.float32)
    gap = ROPE_PARTNER_SHIFT - ROPE_HALF
    ones, zeros = np.ones((seq, gap), np.float32), np.zeros((seq, gap), np.float32)
    cos_t = np.concatenate([cos, ones, cos, ones], axis=1)
    sin_t = np.concatenate([-sin, zeros, sin, zeros], axis=1)
    cos_t = np.concatenate([cos_t, np.ones((identity_rows, HEAD_DIM), np.float32)], axis=0)
    sin_t = np.concatenate([sin_t, np.zeros((identity_rows, HEAD_DIM), np.float32)], axis=0)
    return jnp.asarray(cos_t), jnp.asarray(sin_t)


def _rotary_head_layout(w):
    d_in, width = w.shape
    w = w.reshape(d_in, width // HEAD_DIM, HEAD_DIM)
    split = ROPE_DIM + ROPE_PARTNER_SHIFT - ROPE_HALF
    w = jnp.concatenate([w[..., :ROPE_HALF], w[..., ROPE_DIM:split], w[..., ROPE_HALF:ROPE_DIM], w[..., split:]],
                        axis=-1)
    return w.reshape(d_in, width)


def kernel(x, norm1_g, w_in, gf_up, gf_b, gb_up, gb_b, gla_norm_g, attn_norm_g, w_out, norm2_g, w_gate, w_up,
           conv_w, conv_b, w_down, final_norm_g):
    batch, seq, d_model = x.shape
    depth = norm1_g.shape[0]
    aw, kw, vw = ATTN_WIDTH, GLA_KEY_WIDTH, GLA_VALUE_WIDTH
    assert d_model == D_MODEL and w_in.shape[-1] == 3 * aw + 2 * kw + 2 * vw + 2 * GATE_RANK
    assert all((window // 2) // dil == N_SIDE for window, dil in DILATED_PATTERNS) and DILATIONS == (1, 4, 16)
    assert seq % min(ATTN_TILE, seq) == 0 and seq % (max(DILATIONS) * ATTN_QBLK) == 0

    cos_t, sin_t = _rope_tables(seq, min(INPROJ_ROWS, seq))
    c_gq, c_gk, c_gv, c_gr, c_z = 3 * aw, 3 * aw + kw, 3 * aw + 2 * kw, 3 * aw + 2 * kw + vw, 3 * aw + 2 * kw + 2 * vw

    h2 = x.reshape(batch * seq, D_MODEL)
    for l in range(depth):
        wi = w_in[l]
        w_main = jnp.concatenate([_rotary_head_layout(wi[:, :2 * aw]), wi[:, 2 * aw:c_gq], wi[:, c_gv:c_gr],
                                  wi[:, c_gr:c_z], wi[:, c_gq:c_gk]], axis=1).astype(BF16)
        w_kz = jnp.concatenate([wi[:, c_gk:c_gv], wi[:, c_z:]], axis=1).T.astype(BF16)

        proj, proj4, proj16, kt, z = _inproj(
            h2, norm1_g[l].reshape(1, D_MODEL), w_main, w_kz, cos_t, sin_t, batch, seq)
        by_residue = {1: proj.reshape(batch, 1, seq, PROJ_WIDTH), 4: proj4, 16: proj16}
        branches = [_attention_branch(by_residue[dil], batch, seq, dil) for dil in DILATIONS]
        o_f, o_b = _gla(proj, kt, z, gf_up[l], gf_b[l], gb_up[l], gb_b[l], batch, seq)
        h2 = _outproj(branches, o_f, o_b, proj, h2, attn_norm_g[l].reshape(1, ATTN_WIDTH),
                      gla_norm_g[l].reshape(1, GLA_DV), w_out[l].astype(BF16))
        h2 = _ffn(h2, norm2_g[l].reshape(1, D_MODEL), w_gate[l].astype(BF16), w_up[l].astype(BF16), conv_w[l],
                  conv_b[l].reshape(1, D_FF), w_down[l].astype(BF16), final_norm_g.reshape(1, D_MODEL), seq,
                  final_norm=(l == depth - 1))
    return h2.reshape(batch, seq, D_MODEL)
```

```python
import functools

import jax
import jax.numpy as jnp
import numpy as np
from jax import lax
from jax.experimental import pallas as pl
from jax.experimental.pallas import tpu as pltpu

F32 = jnp.float32
BF16 = jnp.bfloat16

D_MODEL = 2048
ATTN_WIDTH = 1024
HEAD_DIM = 128
ATTN_HEADS = ATTN_WIDTH // HEAD_DIM
DILATED_PATTERNS = ((128, 1), (512, 4), (2048, 16))
DILATIONS = tuple(d for _, d in DILATED_PATTERNS)
N_SIDE = 64
ROPE_THETA = 500000.0
ROPE_DIM = HEAD_DIM // 4
ROPE_HALF = ROPE_DIM // 2
GLA_HEADS = 4
GLA_DK = 128
GLA_DV = 256
GLA_KEY_WIDTH = GLA_HEADS * GLA_DK
GLA_VALUE_WIDTH = GLA_HEADS * GLA_DV
GATE_RANK = 16
GATE_NORMALIZER = 16.0
GLA_CHUNK = 64
D_FF = 5632
CONV_WIDTH = 3
EPS = 1e-6

LANES = 128
BF16_SUBLANES = 16
VMEM_LIMIT_BYTES = 60 * 1024 * 1024

INPROJ_ROWS = 1024
INPROJ_COLS = 512
INPROJ_NORM_CHUNKS = 4
ATTN_TILE = 4096
ATTN_QBLK = 128
ATTN_HEAD_GROUP = 4
ATTN_GROUP_WIDTH = ATTN_HEAD_GROUP * HEAD_DIM
ATTN_GROUPS = ATTN_HEADS // ATTN_HEAD_GROUP
GLA_ROWS = 2048
GLA_PAIR = 2 * GLA_CHUNK
GLA_BATCHES = 2
OUTPROJ_ROWS = 512
FFN_ROWS = 1024
FFN_COLS = 512
FFN_HALO = BF16_SUBLANES

ATTN_PROJ_WIDTH = 3 * ATTN_WIDTH
PROJ_WIDTH = ATTN_PROJ_WIDTH + 2 * GLA_VALUE_WIDTH + GLA_KEY_WIDTH
N_COL_TILES = PROJ_WIDTH // INPROJ_COLS
N_ATTN_COL_TILES = ATTN_PROJ_WIDTH // INPROJ_COLS
RESIDUE_WIDTH = ATTN_PROJ_WIDTH
GLA_V_COL = ATTN_PROJ_WIDTH
GLA_GATE_COL = GLA_V_COL + GLA_VALUE_WIDTH
GLA_Q_COL = GLA_GATE_COL + GLA_VALUE_WIDTH
ROPE_PARTNER_SHIFT = HEAD_DIM // 2

_NT = (((1,), (1,)), ((), ()))


def _params(semantics):
    return pltpu.CompilerParams(dimension_semantics=semantics, vmem_limit_bytes=VMEM_LIMIT_BYTES)


def _bdot(a, b):
    return jnp.dot(a, b, preferred_element_type=F32)


def _inproj_kernel(x_ref, g_ref, w_ref, wkz_ref, cos_ref, sin_ref,
                   p_ref, p4_ref, p16_ref, kt_ref, z_ref, n_ref, rs_a, rs_b, g4_ref):
    i = pl.program_id(0)
    j = pl.program_id(1)
    rows = x_ref.shape[0]
    heads_per_tile = INPROJ_COLS // HEAD_DIM
    live = i < pl.num_programs(0) - 1

    @pl.when((i == 0) & (j == 0))
    def _():
        rs_b[...] = jnp.zeros_like(rs_b)

    @pl.when((j == 0) & live)
    def _():
        chunk = rows // INPROJ_NORM_CHUNKS
        for c in range(INPROJ_NORM_CHUNKS):
            rs = slice(c * chunk, (c + 1) * chunk)
            x = x_ref[rs, :]
            inv = lax.rsqrt(jnp.mean(x * x, axis=-1, keepdims=True) + EPS)
            n = (x * inv * g_ref[...]).astype(BF16)
            n_ref[rs, :] = n
            kz = lax.dot_general(wkz_ref[...], n, _NT, preferred_element_type=F32)
            kt_ref[:, rs] = kz[0:GLA_KEY_WIDTH, :].astype(BF16)
            z_ref[rs, :] = kz[GLA_KEY_WIDTH:, :].T

    done = jnp.where(j == 0, N_COL_TILES - 1, j - 1)
    scaled = (done < ATTN_WIDTH // INPROJ_COLS) | (done == N_COL_TILES - 1)
    scale = jnp.where(scaled, HEAD_DIM ** -0.5, 1.0).astype(F32)

    def step(cur, prev, regroup):
        cos = cos_ref[...]
        sin = sin_ref[...]
        for h in range(heads_per_tile):
            sl = slice(h * HEAD_DIM, (h + 1) * HEAD_DIM)
            t = prev[h]
            r = (t * cos + pltpu.roll(t, ROPE_PARTNER_SHIFT, 1) * sin) * scale
            p_ref[:, sl] = r.astype(BF16)
            if regroup:
                prev[h] = r
        if regroup:
            per4 = rows // 4
            per16 = rows // 16
            for h in range(heads_per_tile):
                sl = slice(h * HEAD_DIM, (h + 1) * HEAD_DIM)
                for res4 in range(4):
                    grp = prev[h, pl.ds(res4, per4, stride=4), :]
                    p4_ref[0, res4, :, sl] = grp.astype(BF16)
                    g4_ref[h, res4 * per4:(res4 + 1) * per4, :] = grp
                for res4 in range(4):
                    for a in range(4):
                        p16_ref[0, 4 * a + res4, :, sl] = (
                            g4_ref[h, pl.ds(res4 * per4 + a, per16, stride=4), :].astype(BF16))
        acc = _bdot(n_ref[...], w_ref[...])
        for h in range(heads_per_tile):
            cur[h] = acc[:, h * HEAD_DIM:(h + 1) * HEAD_DIM]

    attn_tile = (j >= 1) & (j <= N_ATTN_COL_TILES)
    parity = (i * N_COL_TILES + j) % 2
    work = live | (j == 0)
    for pv, (cur, prev) in enumerate(((rs_a, rs_b), (rs_b, rs_a))):
        for regroup in (True, False):
            @pl.when((parity == pv) & work & (attn_tile if regroup else jnp.logical_not(attn_tile)))
            def _(cur=cur, prev=prev, regroup=regroup):
                step(cur, prev, regroup)


def _inproj(x2, g1, w_main, w_kz, cos_t, sin_t, batch, seq):
    t_rows = x2.shape[0]
    tm = min(INPROJ_ROWS, seq)
    tps = seq // tm
    n_rows = t_rows // tm
    last_col = N_COL_TILES - 1
    row = lambda i: jnp.minimum(i, n_rows - 1)
    n_rotary_tiles = 2 * ATTN_WIDTH // INPROJ_COLS
    table_spec = pl.BlockSpec(
        (tm, HEAD_DIM), lambda i, j: (jnp.where((j >= 1) & (j <= n_rotary_tiles), row(i) % tps, tps), 0))

    def finished(i, j):
        r = jnp.where(j == 0, jnp.maximum(i - 1, 0), row(i))
        c = jnp.where(j == 0, jnp.where(i == 0, 0, last_col), jnp.where(i == n_rows, last_col, j - 1))
        return r, c

    def residue_spec(dil):
        def index(i, j):
            c = jnp.where(i == n_rows, N_ATTN_COL_TILES - 1, jnp.clip(j - 1, 0, N_ATTN_COL_TILES - 1))
            return row(i) // tps, 0, row(i) % tps, c
        return pl.BlockSpec((1, dil, tm // dil, INPROJ_COLS), index)

    slab = pltpu.VMEM((INPROJ_COLS // HEAD_DIM, tm, HEAD_DIM), F32)
    return pl.pallas_call(
        _inproj_kernel,
        name="inproj",
        grid=(n_rows + 1, N_COL_TILES),
        in_specs=[
            pl.BlockSpec((tm, D_MODEL), lambda i, j: (row(i), 0)),
            pl.BlockSpec((1, D_MODEL), lambda i, j: (0, 0)),
            pl.BlockSpec((D_MODEL, INPROJ_COLS), lambda i, j: (0, j)),
            pl.BlockSpec((GLA_KEY_WIDTH + 2 * GATE_RANK, D_MODEL), lambda i, j: (0, 0)),
            table_spec,
            table_spec,
        ],
        out_specs=[
            pl.BlockSpec((tm, INPROJ_COLS), finished),
            residue_spec(4),
            residue_spec(16),
            pl.BlockSpec((GLA_KEY_WIDTH, tm), lambda i, j: (0, row(i))),
            pl.BlockSpec((tm, 2 * GATE_RANK), lambda i, j: (row(i), 0)),
        ],
        out_shape=[
            jax.ShapeDtypeStruct((t_rows, PROJ_WIDTH), BF16),
            jax.ShapeDtypeStruct((batch, 4, seq // 4, RESIDUE_WIDTH), BF16),
            jax.ShapeDtypeStruct((batch, 16, seq // 16, RESIDUE_WIDTH), BF16),
            jax.ShapeDtypeStruct((GLA_KEY_WIDTH, t_rows), BF16),
            jax.ShapeDtypeStruct((t_rows, 2 * GATE_RANK), F32),
        ],
        scratch_shapes=[pltpu.VMEM((tm, D_MODEL), BF16), slab, slab, slab],
        compiler_params=_params(("arbitrary", "arbitrary")),
    )(x2, g1, w_main, w_kz, cos_t, sin_t)


def _attn_kernel(q_ref, kp_ref, kc_ref, kn_ref, vp_ref, vc_ref, vn_ref, o_ref, lse_ref, osc, *, length, dil):
    t = pl.program_id(2)
    tq = q_ref.shape[2]
    win = ATTN_QBLK + 2 * N_SIDE
    n_blk = tq // ATTN_QBLK

    def window(prev_ref, cur_ref, next_ref, res, blk, cols):
        lo, hi = blk * ATTN_QBLK - N_SIDE, blk * ATTN_QBLK + ATTN_QBLK + N_SIDE
        parts = []
        if lo < 0:
            parts.append(prev_ref[0, res, :, cols])
        parts.append(cur_ref[0, res, max(lo, 0):min(hi, tq), cols])
        if hi > tq:
            parts.append(next_ref[0, res, :, cols])
        return parts[0] if len(parts) == 1 else jnp.concatenate(parts, axis=0)

    ones = jnp.ones((win, HEAD_DIM), BF16)

    row = lax.broadcasted_iota(jnp.int32, (ATTN_QBLK, win), 0)
    col = lax.broadcasted_iota(jnp.int32, (ATTN_QBLK, win), 1)
    band = (col >= row) & (col <= row + 2 * N_SIDE)
    lane = lax.broadcasted_iota(jnp.int32, (ATTN_QBLK, LANES), 1)

    for res, blk in [(a, b) for a in range(dil) for b in range(n_blk)]:
        r0 = blk * ATTN_QBLK
        first_key = t * tq + r0 - N_SIDE
        if 0 < blk < n_blk - 1:
            valid = band
        else:
            valid = band & (col >= -first_key) & (col < length - first_key)
        if dil == 1:
            dst = pl.ds(r0, ATTN_QBLK)
        else:
            dst = pl.ds(r0 * dil + res, ATTN_QBLK, stride=dil)
        lse_tile = jnp.zeros((ATTN_QBLK, LANES), F32)
        for h in range(ATTN_HEAD_GROUP):
            hs = slice(h * HEAD_DIM, (h + 1) * HEAD_DIM)
            q = q_ref[0, res, pl.ds(r0, ATTN_QBLK), hs]
            k = window(kp_ref, kc_ref, kn_ref, res, blk, hs)
            v_ones = jnp.concatenate([window(vp_ref, vc_ref, vn_ref, res, blk, hs), ones], axis=1)
            s = lax.dot_general(q, k, _NT, preferred_element_type=F32)
            s = jnp.where(valid, s, -1e30)
            m = jnp.max(s, axis=-1, keepdims=True)
            p = jnp.exp(s - m)
            o_den = _bdot(p.astype(BF16), v_ones)
            den = o_den[:, HEAD_DIM:]
            osc[h, dst, :] = o_den[:, :HEAD_DIM] / den
            lse_tile = jnp.where(lane == h, m + jnp.log(den), lse_tile)
        lse_ref[dst, :] = lse_tile

    for h in range(ATTN_HEAD_GROUP):
        o_ref[:, h * HEAD_DIM:(h + 1) * HEAD_DIM] = osc[h].astype(o_ref.dtype)


def _attention_branch(proj_res, batch, seq, dil):
    length = seq // dil
    tn = min(ATTN_TILE, seq)
    tq = tn // dil
    n_tiles = seq // tn
    halo_per_tile = tq // N_SIDE
    n_halo_blocks = length // N_SIDE
    gw = ATTN_GROUP_WIDTH
    part_blocks = ATTN_WIDTH // gw

    def cur(part):
        return pl.BlockSpec((1, dil, tq, gw), lambda b, g, t: (b, 0, t, part * part_blocks + g))

    def prev(part):
        return pl.BlockSpec((1, dil, N_SIDE, gw),
                            lambda b, g, t: (b, 0, jnp.maximum(t * halo_per_tile - 1, 0), part * part_blocks + g))

    def nxt(part):
        return pl.BlockSpec((1, dil, N_SIDE, gw),
                            lambda b, g, t: (b, 0, jnp.minimum((t + 1) * halo_per_tile, n_halo_blocks - 1),
                                             part * part_blocks + g))

    return pl.pallas_call(
        functools.partial(_attn_kernel, length=length, dil=dil),
        name=f"attn_d{dil}",
        grid=(batch, ATTN_GROUPS, n_tiles),
        in_specs=[cur(0), prev(1), cur(1), nxt(1), prev(2), cur(2), nxt(2)],
        out_specs=[
            pl.BlockSpec((tn, gw), lambda b, g, t: (b * n_tiles + t, g)),
            pl.BlockSpec((tn, LANES), lambda b, g, t: (b * n_tiles + t, g)),
        ],
        out_shape=[
            jax.ShapeDtypeStruct((batch * seq, ATTN_WIDTH), BF16),
            jax.ShapeDtypeStruct((batch * seq, ATTN_GROUPS * LANES), F32),
        ],
        scratch_shapes=[pltpu.VMEM((ATTN_HEAD_GROUP, tn, HEAD_DIM), F32)],
        compiler_params=_params(("parallel", "parallel", "parallel")),
    )(proj_res, proj_res, proj_res, proj_res, proj_res, proj_res, proj_res)


def _split2(a):
    hi = a.astype(BF16)
    lo = (a - hi.astype(F32)).astype(BF16)
    return hi, lo


def _dot_f32(a, b):
    a_hi, a_lo = _split2(a)
    b_hi, b_lo = _split2(b)
    return _bdot(a_hi, b_hi) + (_bdot(a_hi, b_lo) + _bdot(a_lo, b_hi))


def _log_sigmoid(x):
    return jnp.minimum(x, 0.0) - jnp.log(1.0 + jnp.exp(-jnp.abs(x)))


class _Direction:
    def __init__(self, refs, forward):
        self.q_ref, self.kt_ref, self.v_ref, z_ref, up_ref, bias_ref, self.o_ref, self.state_ref = refs
        c = GLA_CHUNK
        r = lax.broadcasted_iota(jnp.int32, (GLA_PAIR, GLA_PAIR), 0)
        cc = lax.broadcasted_iota(jnp.int32, (GLA_PAIR, GLA_PAIR), 1)
        same = jnp.where(r < c, 0, 1) == jnp.where(cc < c, 0, 1)
        lower = same & (r >= cc)
        upper = same & (r <= cc)
        self.row_low = r < c
        self.col_low = cc < c
        if forward:
            left, self.causal = lower, lower
            self.ref_i, self.last_i = c // 2 - 1, c - 1
            self.cross = (r >= c) & (cc < c)
            self.i_first, self.i_second = self.last_i, c + self.last_i
            self.second_rows, self.first_cols = r >= c, cc < c
        else:
            left, self.causal = upper, upper
            self.ref_i, self.last_i = c // 2, 0
            self.cross = (r < c) & (cc >= c)
            self.i_first, self.i_second = c + self.last_i, self.last_i
            self.second_rows, self.first_cols = r < c, cc >= c
        left_m = jnp.where(left, 1.0, 0.0).astype(BF16)
        self.left2_m = jnp.concatenate([left_m, left_m], axis=1)
        eye = jnp.where(r == cc, 1.0, 0.0).astype(BF16)
        self.eye2 = jnp.concatenate([eye, eye], axis=1)
        inv_norm = 1.0 / GATE_NORMALIZER
        self.g = _log_sigmoid(_dot_f32(z_ref[...], up_ref[...]) + bias_ref[...]) * inv_norm


def _gla_kernel(qf_ref, ktf_ref, vf_ref, zf_ref, qb_ref, ktb_ref, vb_ref, zb_ref,
                upf_ref, bf_ref, upb_ref, bb_ref, of_ref, ob_ref, sf_ref, sb_ref):
    @pl.when(pl.program_id(2) == 0)
    def _():
        sf_ref[...] = jnp.zeros_like(sf_ref)
        sb_ref[...] = jnp.zeros_like(sb_ref)

    c = GLA_CHUNK
    ts = qf_ref.shape[0]
    n_pairs = ts // GLA_PAIR
    fwd = _Direction((qf_ref, ktf_ref, vf_ref, zf_ref, upf_ref, bf_ref, of_ref, sf_ref), True)
    bwd = _Direction((qb_ref, ktb_ref, vb_ref, zb_ref, upb_ref, bb_ref, ob_ref, sb_ref), False)
    all_jobs = []
    for s in range(n_pairs):
        all_jobs.append((fwd, slice(s * GLA_PAIR, (s + 1) * GLA_PAIR)))
        all_jobs.append((bwd, slice((n_pairs - 1 - s) * GLA_PAIR, (n_pairs - s) * GLA_PAIR)))

    state = {id(fwd): sf_ref[...], id(bwd): sb_ref[...]}
    per_batch = len(all_jobs) // GLA_BATCHES
    for first in range(0, len(all_jobs), per_batch):
        jobs = all_jobs[first:first + per_batch]

        cums = []
        for d, rows in jobs:
            g_hi, g_lo = _split2(d.g[rows])
            b = _bdot(d.left2_m, jnp.concatenate([g_hi, g_lo], axis=0))
            b_hi, b_lo = _split2(b)
            bt = lax.dot_general(d.eye2, jnp.concatenate([b_hi, b_lo], axis=1), _NT, preferred_element_type=F32)
            cums.append((b, bt))

        ops = []
        for (d, rows), (b, bt) in zip(jobs, cums):
            def rowpick(a, i):
                return jnp.where(d.row_low, a[i:i + 1, :], a[c + i:c + i + 1, :])

            def colpick(a, i):
                return jnp.where(d.col_low, a[:, i:i + 1], a[:, c + i:c + i + 1])

            q = d.q_ref[rows, :].astype(F32)
            kt = d.kt_ref[:, rows].astype(F32)
            qe = (q * jnp.exp(b - rowpick(b, d.ref_i))).astype(BF16)
            q_in = q * jnp.exp(b)
            ket = (kt * jnp.exp(colpick(bt, d.ref_i) - bt)).astype(BF16)
            kst = kt * jnp.exp(colpick(bt, d.last_i) - bt)
            dec_first_row = jnp.exp(b[d.i_first:d.i_first + 1, :])
            dec_second_col = jnp.exp(bt[:, d.i_second:d.i_second + 1])
            dec_total_col = jnp.exp(bt[:, d.i_first:d.i_first + 1] + bt[:, d.i_second:d.i_second + 1])
            q_state = (q_in * jnp.where(d.second_rows, dec_first_row, 1.0)).astype(BF16)
            k_state = (kst * jnp.where(d.first_cols, dec_second_col, 1.0)).astype(BF16)
            ops.append((qe, ket, q_in.astype(BF16), kst.astype(BF16), q_state, k_state, dec_total_col))

        scores = []
        for qe, ket, q_in, kst, _, _, _ in ops:
            scores.append((_bdot(qe, ket), _bdot(q_in, kst)))

        locals_ = []
        for (d, rows), (a_diag, a_cross), op in zip(jobs, scores, ops):
            a = jnp.where(d.causal, a_diag, jnp.where(d.cross, a_cross, 0.0)).astype(BF16)
            v = d.v_ref[rows, :]
            locals_.append((jnp.concatenate([a, op[4]], axis=1), v, _bdot(op[5], v)))

        for (d, rows), (a_q, v, upd), op in zip(jobs, locals_, ops):
            s = state[id(d)]
            o = _bdot(a_q, jnp.concatenate([v, s.astype(BF16)], axis=0))
            d.o_ref[rows, :] = o.astype(d.o_ref.dtype)
            state[id(d)] = op[6] * s + upd
    sf_ref[...] = state[id(fwd)]
    sb_ref[...] = state[id(bwd)]


def _gla(proj, kt, z, gf_up, gf_b, gb_up, gb_b, batch, seq):
    t_rows = batch * seq
    ts = min(GLA_ROWS, seq)
    n_s = seq // ts
    q_col0 = GLA_Q_COL // GLA_DK
    v_col0 = GLA_V_COL // GLA_DV

    zero = jnp.zeros((GATE_RANK, GLA_KEY_WIDTH), F32)
    upf = jnp.concatenate([gf_up, zero], axis=0)
    upb = jnp.concatenate([zero, gb_up], axis=0)
    bf = gf_b.reshape(1, GLA_KEY_WIDTH)
    bb = gb_b.reshape(1, GLA_KEY_WIDTH)

    def tile_specs(tile):
        return [
            pl.BlockSpec((ts, GLA_DK), lambda b, h, i: (tile(b, i), q_col0 + h)),
            pl.BlockSpec((GLA_DK, ts), lambda b, h, i: (h, tile(b, i))),
            pl.BlockSpec((ts, GLA_DV), lambda b, h, i: (tile(b, i), v_col0 + h)),
            pl.BlockSpec((ts, 2 * GATE_RANK), lambda b, h, i: (tile(b, i), 0)),
        ]

    fwd_tile = lambda b, i: b * n_s + i
    bwd_tile = lambda b, i: b * n_s + (n_s - 1 - i)
    gate_specs = [
        pl.BlockSpec((2 * GATE_RANK, GLA_DK), lambda b, h, i: (0, h)),
        pl.BlockSpec((1, GLA_DK), lambda b, h, i: (0, h)),
    ]
    return pl.pallas_call(
        _gla_kernel,
        name="gla",
        grid=(batch, GLA_HEADS, n_s),
        in_specs=tile_specs(fwd_tile) + tile_specs(bwd_tile) + gate_specs + gate_specs,
        out_specs=[
            pl.BlockSpec((ts, GLA_DV), lambda b, h, i: (fwd_tile(b, i), h)),
            pl.BlockSpec((ts, GLA_DV), lambda b, h, i: (bwd_tile(b, i), h)),
        ],
        out_shape=[jax.ShapeDtypeStruct((t_rows, GLA_VALUE_WIDTH), BF16)] * 2,
        scratch_shapes=[pltpu.VMEM((GLA_DK, GLA_DV), F32), pltpu.VMEM((GLA_DK, GLA_DV), F32)],
        compiler_params=_params(("parallel", "parallel", "arbitrary")),
    )(proj, kt, proj, z, proj, kt, proj, z, upf, bf, upb, bb)


def _outproj_kernel(o1_ref, o2_ref, o3_ref, l1_ref, l2_ref, l3_ref, of_ref, ob_ref, gr_ref, x_ref,
                    ag_ref, gg_ref, w_ref, h_ref, ao_ref, cat_a, cat_b):
    i = pl.program_id(0)
    rows = x_ref.shape[0]

    @pl.when(i == 0)
    def _():
        cat_b[...] = jnp.zeros_like(cat_b)

    def step(cur, prev):
        l1, l2, l3 = l1_ref[...], l2_ref[...], l3_ref[...]
        m = jnp.maximum(jnp.maximum(l1, l2), l3)
        e1, e2, e3 = jnp.exp(l1 - m), jnp.exp(l2 - m), jnp.exp(l3 - m)
        inv = 1.0 / (e1 + e2 + e3)
        w1, w2, w3 = e1 * inv, e2 * inv, e3 * inv

        ssq = jnp.zeros((rows, 1), F32)
        for h in range(ATTN_HEADS):
            hs = slice(h * HEAD_DIM, (h + 1) * HEAD_DIM)
            lc = (h // ATTN_HEAD_GROUP) * LANES + h % ATTN_HEAD_GROUP
            ao = (w1[:, lc:lc + 1] * o1_ref[:, hs].astype(F32) + w2[:, lc:lc + 1] * o2_ref[:, hs].astype(F32)
                  + w3[:, lc:lc + 1] * o3_ref[:, hs].astype(F32))
            ssq = ssq + jnp.sum(ao * ao, axis=-1, keepdims=True)
            ao_ref[:, hs] = ao
        inv_rms = lax.rsqrt(ssq * (1.0 / ATTN_WIDTH) + EPS)
        cur[:, 0:ATTN_WIDTH] = (ao_ref[...] * inv_rms * ag_ref[...]).astype(BF16)

        for h in range(GLA_HEADS):
            hs = slice(h * GLA_DV, (h + 1) * GLA_DV)
            o = of_ref[:, hs].astype(F32) + ob_ref[:, hs].astype(F32)
            y = o * lax.rsqrt(jnp.mean(o * o, axis=-1, keepdims=True) + EPS) * gg_ref[...]
            gate = gr_ref[:, hs].astype(F32)
            y = y * (gate / (1.0 + jnp.exp(-gate)))
            cur[:, ATTN_WIDTH + h * GLA_DV:ATTN_WIDTH + (h + 1) * GLA_DV] = y.astype(BF16)

        h_ref[...] = x_ref[...] + _bdot(prev[...], w_ref[...])

    @pl.when(i % 2 == 0)
    def _():
        step(cat_a, cat_b)

    @pl.when(i % 2 == 1)
    def _():
        step(cat_b, cat_a)


def _outproj(branches, o_f, o_b, proj, x2, attn_g, gla_g, w_out):
    t_rows = x2.shape[0]
    tm = OUTPROJ_ROWS
    n_tiles = t_rows // tm
    mixed = lambda i: jnp.minimum(i, n_tiles - 1)
    projected = lambda i: jnp.maximum(i - 1, 0)
    mix = lambda width: pl.BlockSpec((tm, width), lambda i: (mixed(i), 0))
    (o1, l1), (o2, l2), (o3, l3) = branches
    lse_w = ATTN_GROUPS * LANES
    cat = pltpu.VMEM((tm, D_MODEL), BF16)
    return pl.pallas_call(
        _outproj_kernel,
        name="outproj",
        grid=(n_tiles + 1,),
        in_specs=[
            mix(ATTN_WIDTH), mix(ATTN_WIDTH), mix(ATTN_WIDTH), mix(lse_w), mix(lse_w), mix(lse_w),
            mix(GLA_VALUE_WIDTH), mix(GLA_VALUE_WIDTH),
            pl.BlockSpec((tm, GLA_VALUE_WIDTH), lambda i: (mixed(i), GLA_GATE_COL // GLA_VALUE_WIDTH)),
            pl.BlockSpec((tm, D_MODEL), lambda i: (projected(i), 0)),
            pl.BlockSpec((1, ATTN_WIDTH), lambda i: (0, 0)),
            pl.BlockSpec((1, GLA_DV), lambda i: (0, 0)),
            pl.BlockSpec((D_MODEL, D_MODEL), lambda i: (0, 0), pipeline_mode=pl.Buffered(1)),
        ],
        out_specs=pl.BlockSpec((tm, D_MODEL), lambda i: (projected(i), 0)),
        out_shape=jax.ShapeDtypeStruct((t_rows, D_MODEL), F32),
        scratch_shapes=[pltpu.VMEM((tm, ATTN_WIDTH), F32), cat, cat],
        compiler_params=_params(("arbitrary",)),
    )(o1, o2, o3, l1, l2, l3, o_f, o_b, proj, x2, attn_g, gla_g, w_out)


def _ffn_kernel(h_ref, hp_ref, hn_ref, g2_ref, wg_ref, wu_ref, cw_ref, cb_ref, wd_ref, gf_ref,
                out_ref, n_ref, *, tiles_per_seq, final_norm):
    i = pl.program_id(0)
    j = pl.program_id(1)
    tm = h_ref.shape[0]
    halo = FFN_HALO

    def norm(x):
        return (x * lax.rsqrt(jnp.mean(x * x, axis=-1, keepdims=True) + EPS) * g2_ref[...]).astype(BF16)

    @pl.when(j == 0)
    def _():
        n_ref[0:halo] = norm(hp_ref[...])
        n_ref[halo:halo + tm] = norm(h_ref[...])
        n_ref[halo + tm:] = norm(hn_ref[...])
        out_ref[...] = h_ref[...]

    gate = _bdot(n_ref[...], wg_ref[...])
    ext = tm + 2 * halo
    row = lax.broadcasted_iota(jnp.int32, (tm, 1), 0)
    seq_tile = i % tiles_per_seq
    g_prev = jnp.where((row == 0) & (seq_tile == 0), 0.0, pltpu.roll(gate, 1, 0)[halo:halo + tm, :])
    g_next = jnp.where((row == tm - 1) & (seq_tile == tiles_per_seq - 1), 0.0,
                       pltpu.roll(gate, ext - 1, 0)[halo:halo + tm, :])
    g_mid = gate[halo:halo + tm, :]
    cw = cw_ref[...]
    conv = g_prev * cw[0:1, :] + g_mid * cw[1:2, :] + g_next * cw[2:3, :] + cb_ref[...]
    up = _bdot(n_ref[halo:halo + tm, :], wu_ref[...])
    act = (conv / (1.0 + jnp.exp(-conv))) * up
    out_ref[...] += _bdot(act.astype(BF16), wd_ref[...])

    if final_norm:
        @pl.when(j == pl.num_programs(1) - 1)
        def _():
            y = out_ref[...]
            out_ref[...] = y * lax.rsqrt(jnp.mean(y * y, axis=-1, keepdims=True) + EPS) * gf_ref[...]


def _ffn(h2, g2, wg, wu, conv_w, conv_b, wd, gfinal, seq, final_norm):
    t_rows = h2.shape[0]
    tm = min(FFN_ROWS, seq)
    tf = FFN_COLS
    halo_blocks_per_tile = tm // FFN_HALO
    n_halo_blocks = t_rows // FFN_HALO
    return pl.pallas_call(
        functools.partial(_ffn_kernel, tiles_per_seq=seq // tm, final_norm=final_norm),
        name="ffn",
        grid=(t_rows // tm, D_FF // tf),
        in_specs=[
            pl.BlockSpec((tm, D_MODEL), lambda i, j: (i, 0)),
            pl.BlockSpec((FFN_HALO, D_MODEL), lambda i, j: (jnp.maximum(i * halo_blocks_per_tile - 1, 0), 0)),
            pl.BlockSpec((FFN_HALO, D_MODEL),
                         lambda i, j: (jnp.minimum((i + 1) * halo_blocks_per_tile, n_halo_blocks - 1), 0)),
            pl.BlockSpec((1, D_MODEL), lambda i, j: (0, 0)),
            pl.BlockSpec((D_MODEL, tf), lambda i, j: (0, j)),
            pl.BlockSpec((D_MODEL, tf), lambda i, j: (0, j)),
            pl.BlockSpec((CONV_WIDTH, tf), lambda i, j: (0, j)),
            pl.BlockSpec((1, tf), lambda i, j: (0, j)),
            pl.BlockSpec((tf, D_MODEL), lambda i, j: (j, 0)),
            pl.BlockSpec((1, D_MODEL), lambda i, j: (0, 0)),
        ],
        out_specs=pl.BlockSpec((tm, D_MODEL), lambda i, j: (i, 0)),
        out_shape=jax.ShapeDtypeStruct((t_rows, D_MODEL), F32),
        scratch_shapes=[pltpu.VMEM((tm + 2 * FFN_HALO, D_MODEL), BF16)],
        compiler_params=_params(("parallel", "arbitrary")),
    )(h2, h2, h2, g2, wg, wu, conv_w, conv_b, wd, gfinal)


def _rope_tables(seq, identity_rows):
    pos = np.arange(seq, dtype=np.float32)
    inv_freq = np.float32(ROPE_THETA) ** (-np.arange(0, ROPE_DIM, 2, dtype=np.float32) / np.float32(ROPE_DIM))
    ang = (pos[:, None] * inv_freq.astype(np.float32)[None, :]).astype(np.float32)
    cos = np.cos(ang.astype(np.float64)).astype(np.float32)
    sin = np.sin(ang.astype(np.float64)).astype(np.float32)
    gap = ROPE_PARTNER_SHIFT - ROPE_HALF
    ones, zeros = np.ones((seq, gap), np.float32), np.zeros((seq, gap), np.float32)
    cos_t = np.concatenate([cos, ones, cos, ones], axis=1)
    sin_t = np.concatenate([-sin, zeros, sin, zeros], axis=1)
    cos_t = np.concatenate([cos_t, np.ones((identity_rows, HEAD_DIM), np.float32)], axis=0)
    sin_t = np.concatenate([sin_t, np.zeros((identity_rows, HEAD_DIM), np.float32)], axis=0)
    return jnp.asarray(cos_t), jnp.asarray(sin_t)


def _rotary_head_layout(w):
    d_in, width = w.shape
    w = w.reshape(d_in, width // HEAD_DIM, HEAD_DIM)
    split = ROPE_DIM + ROPE_PARTNER_SHIFT - ROPE_HALF
    w = jnp.concatenate([w[..., :ROPE_HALF], w[..., ROPE_DIM:split], w[..., ROPE_HALF:ROPE_DIM], w[..., split:]],
                        axis=-1)
    return w.reshape(d_in, width)


def kernel(x, norm1_g, w_in, gf_up, gf_b, gb_up, gb_b, gla_norm_g, attn_norm_g, w_out, norm2_g, w_gate, w_up,
           conv_w, conv_b, w_down, final_norm_g):
    batch, seq, d_model = x.shape
    depth = norm1_g.shape[0]
    aw, kw, vw = ATTN_WIDTH, GLA_KEY_WIDTH, GLA_VALUE_WIDTH
    assert d_model == D_MODEL and w_in.shape[-1] == 3 * aw + 2 * kw + 2 * vw + 2 * GATE_RANK
    assert all((window // 2) // dil == N_SIDE for window, dil in DILATED_PATTERNS) and DILATIONS == (1, 4, 16)
    assert seq % min(ATTN_TILE, seq) == 0 and seq % (max(DILATIONS) * ATTN_QBLK) == 0

    cos_t, sin_t = _rope_tables(seq, min(INPROJ_ROWS, seq))
    c_gq, c_gk, c_gv, c_gr, c_z = 3 * aw, 3 * aw + kw, 3 * aw + 2 * kw, 3 * aw + 2 * kw + vw, 3 * aw + 2 * kw + 2 * vw

    h2 = x.reshape(batch * seq, D_MODEL)
    for l in range(depth):
        wi = w_in[l]
        w_main = jnp.concatenate([_rotary_head_layout(wi[:, :2 * aw]), wi[:, 2 * aw:c_gq], wi[:, c_gv:c_gr],
                                  wi[:, c_gr:c_z], wi[:, c_gq:c_gk]], axis=1).astype(BF16)
        w_kz = jnp.concatenate([wi[:, c_gk:c_gv], wi[:, c_z:]], axis=1).T.astype(BF16)

        proj, proj4, proj16, kt, z = _inproj(
            h2, norm1_g[l].reshape(1, D_MODEL), w_main, w_kz, cos_t, sin_t, batch, seq)
        by_residue = {1: proj.reshape(batch, 1, seq, PROJ_WIDTH), 4: proj4, 16: proj16}
        branches = [_attention_branch(by_residue[dil], batch, seq, dil) for dil in DILATIONS]
        o_f, o_b = _gla(proj, kt, z, gf_up[l], gf_b[l], gb_up[l], gb_b[l], batch, seq)
        h2 = _outproj(branches, o_f, o_b, proj, h2, attn_norm_g[l].reshape(1, ATTN_WIDTH),
                      gla_norm_g[l].reshape(1, GLA_DV), w_out[l].astype(BF16))
        h2 = _ffn(h2, norm2_g[l].reshape(1, D_MODEL), w_gate[l].astype(BF16), w_up[l].astype(BF16), conv_w[l],
                  conv_b[l].reshape(1, D_FF), w_down[l].astype(BF16), final_norm_g.reshape(1, D_MODEL), seq,
                  final_norm=(l == depth - 1))
    return h2.reshape(batch, seq, D_MODEL)
```
